```python
import functools
import jax, jax.numpy as jnp
from jax import lax
import numpy as np

D_MODEL = 1024
BATCH = 4
SEQ = 8192
DEPTH = 1
DEC_BATCH = 128
DEC_SEQ = 8
PAST_LEN = 16384
PAGE_SIZE = 128

MLA_HEADS = 8
Q_RANK = 256
KV_RANK = 128
NOPE_DIM = 64
ROPE_DIM = 32
V_DIM = 64
MLA_QK_DIM = NOPE_DIM + ROPE_DIM
MLA_SCALE = MLA_QK_DIM ** -0.5
ROPE_THETA = 10000.0
SB_HEADS = 8
SB_DIM = 64
SB_WIDTH = SB_HEADS * SB_DIM
SB_SCALE = SB_DIM ** -0.5
D_IN = Q_RANK + KV_RANK + ROPE_DIM + 3 * SB_WIDTH + 2 * D_MODEL
BLK = 128
N_EXPERTS = 32
TOP_K = 4
D_FF = D_MODEL
SWIGLU_LIMIT = 7.0
SWIGLU_ALPHA = 1.702
MOE_BLK = 128
EPS = 1e-6
NEG = -1e30

kernel_name = "mla_stickbreak_gated_moe_step"


def rmsnorm(x, g):
    xf = x.astype(jnp.float32)
    y = xf * lax.rsqrt(jnp.mean(xf * xf, axis=-1, keepdims=True) + EPS)
    return (y * g.astype(jnp.float32)).astype(x.dtype)


def modulate(x, g, shift, scale):
    return rmsnorm(x, g) * (1.0 + scale[:, None, :]) + shift[:, None, :]


def rope(x, pos):
    half = ROPE_DIM // 2
    inv = ROPE_THETA ** (-jnp.arange(half, dtype=jnp.float32) / half)
    ang = pos.astype(jnp.float32)[:, None] * inv[None, :]
    shape = (1, pos.shape[0]) + (1,) * (x.ndim - 3) + (half,)
    cos = jnp.cos(ang).reshape(shape)
    sin = jnp.sin(ang).reshape(shape)
    xf = x.astype(jnp.float32)
    x1, x2 = xf[..., :half], xf[..., half:]
    return jnp.concatenate([x1 * cos - x2 * sin, x1 * sin + x2 * cos], axis=-1).astype(x.dtype)


def project(h, pos, w_in, g_qa, w_q_b, g_kva, g_qn, g_qr, g_kr):
    b, t, _ = h.shape
    sizes = (Q_RANK, KV_RANK, ROPE_DIM, SB_WIDTH, SB_WIDTH, SB_WIDTH, D_MODEL)
    cuts = [int(v) for v in np.cumsum(sizes)]
    qa, ckv_raw, kr_raw, qs, ks, vs, ga_l, gb_l = jnp.split(h @ w_in, cuts, axis=-1)
    q = (rmsnorm(qa, g_qa) @ w_q_b).reshape(b, t, MLA_HEADS, MLA_QK_DIM)
    q_m = jnp.concatenate([rmsnorm(q[..., :NOPE_DIM], g_qn),
                           rope(rmsnorm(q[..., NOPE_DIM:], g_qr), pos)], axis=-1) * MLA_SCALE
    ckv = rmsnorm(ckv_raw, g_kva)
    krope = rope(rmsnorm(kr_raw, g_kr), pos)
    q_s = qs.reshape(b, t, SB_HEADS, SB_DIM) * SB_SCALE
    k_s = ks.reshape(b, t, SB_HEADS, SB_DIM)
    v_s = vs.reshape(b, t, SB_HEADS, SB_DIM)
    return q_m, q_s, ckv, krope, k_s, v_s, jax.nn.sigmoid(ga_l), jax.nn.sigmoid(gb_l)


def expand_latent(ckv, krope, w_kv_b, g_kn):
    b, t, _ = ckv.shape
    kv = (ckv @ w_kv_b).reshape(b, t, MLA_HEADS, NOPE_DIM + V_DIM)
    k_nope = rmsnorm(kv[..., :NOPE_DIM], g_kn)
    k_rope = jnp.broadcast_to(krope[:, :, None, :], (b, t, MLA_HEADS, ROPE_DIM)).astype(k_nope.dtype)
    return jnp.concatenate([k_nope, k_rope], axis=-1), kv[..., NOPE_DIM:]


def attn_init(b, tq):
    f32 = jnp.float32
    return (jnp.full((b, MLA_HEADS, tq), NEG, f32),
            jnp.zeros((b, MLA_HEADS, tq), f32),
            jnp.zeros((b, MLA_HEADS, tq, V_DIM), f32),
            jnp.zeros((b, SB_HEADS, tq), f32),
            jnp.zeros((b, SB_HEADS, tq, SB_DIM), f32))


def attn_step(st, q_m, q_s, q_pos, k_m, v_m, k_s, v_s, k_pos):
    m, l, o_m, lsurv, o_s = st
    f32 = jnp.float32
    causal = k_pos[None, :] <= q_pos[:, None]
    s = jnp.einsum('bqhd,bkhd->bhqk', q_m, k_m, preferred_element_type=f32)
    s = jnp.where(causal, s, NEG)
    m_new = jnp.maximum(m, s.max(axis=-1))
    p = jnp.where(causal, jnp.exp(s - m_new[..., None]), 0.0)
    alpha = jnp.exp(m - m_new)
    l = l * alpha + p.sum(axis=-1)
    o_m = o_m * alpha[..., None] + jnp.einsum('bhqk,bkhd->bhqd', p, v_m.astype(f32))
    strict = k_pos[None, :] < q_pos[:, None]
    z = jnp.einsum('bqhd,bkhd->bhqk', q_s, k_s, preferred_element_type=f32)
    log1m = jnp.where(strict, jax.nn.log_sigmoid(-z), 0.0)
    later = lax.cumsum(log1m, axis=3, reverse=True) - log1m
    a = jnp.where(strict, jnp.exp(jax.nn.log_sigmoid(z) + later + lsurv[..., None]), 0.0)
    o_s = o_s + jnp.einsum('bhqk,bkhd->bhqd', a, v_s.astype(f32))
    lsurv = lsurv + log1m.sum(axis=-1)
    return (m_new, l, o_m, lsurv, o_s)


def attn_finish(st):
    _, l, o_m, _, o_s = st
    b, _, tq, _ = o_m.shape
    y_m = (o_m / l[..., None]).transpose(0, 2, 1, 3).reshape(b, tq, MLA_HEADS * V_DIM)
    y_s = o_s.transpose(0, 2, 1, 3).reshape(b, tq, SB_WIDTH)
    return y_m, y_s


def prompt_attention(q_m, q_s, ckv, krope, k_s, v_s, w_kv_b, g_kn):
    b, s = q_m.shape[:2]
    k_m, v_m = expand_latent(ckv, krope, w_kv_b, g_kn)

    def q_block(qb):
        q0 = qb * BLK
        qm = lax.dynamic_slice_in_dim(q_m, q0, BLK, axis=1)
        qs = lax.dynamic_slice_in_dim(q_s, q0, BLK, axis=1)
        q_pos = q0 + jnp.arange(BLK)

        def body(o, st):
            k0 = (qb - o) * BLK
            sl = lambda arr: lax.dynamic_slice_in_dim(arr, k0, BLK, axis=1)
            return attn_step(st, qm, qs, q_pos, sl(k_m), sl(v_m), sl(k_s), sl(v_s), k0 + jnp.arange(BLK))

        return attn_finish(lax.fori_loop(0, qb + 1, body, attn_init(b, BLK)))

    y_m, y_s = lax.map(q_block, jnp.arange(s // BLK))
    y_m = y_m.transpose(1, 0, 2, 3).reshape(b, s, -1)
    y_s = y_s.transpose(1, 0, 2, 3).reshape(b, s, -1)
    return y_m, y_s


def sample_attention(q_m, q_s, ckv, krope, k_s, v_s, cache_ckv, cache_krope, cache_k, cache_v,
                     page_table, layer, w_kv_b, g_kn):
    b, t = q_m.shape[:2]
    q_pos = PAST_LEN + jnp.arange(t)
    k_m, v_m = expand_latent(ckv, krope, w_kv_b, g_kn)
    st = attn_step(attn_init(b, t), q_m, q_s, q_pos, k_m, v_m, k_s, v_s, q_pos)
    n_pages = PAST_LEN // PAGE_SIZE

    def page_body(st, xs):
        lp, phys = xs
        pk_m, pv_m = expand_latent(cache_ckv[layer, phys], cache_krope[layer, phys], w_kv_b, g_kn)
        st = attn_step(st, q_m, q_s, q_pos, pk_m, pv_m, cache_k[layer, phys], cache_v[layer, phys],
                       lp * PAGE_SIZE + jnp.arange(PAGE_SIZE))
        return st, None

    st, _ = lax.scan(page_body, st, (jnp.arange(n_pages)[::-1], page_table.T[::-1]))
    return attn_finish(st)


def moe(h, w_router, b_router, w_gu, b_gu, w_down, b_down):
    n, d = h.shape
    nk = n * TOP_K
    logits = (h @ w_router).astype(jnp.float32) + b_router.astype(jnp.float32)
    top_v, top_i = lax.top_k(logits, TOP_K)
    probs = jax.nn.softmax(top_v, axis=-1)
    flat_e = top_i.reshape(-1)
    flat_t = jnp.repeat(jnp.arange(n, dtype=jnp.int32), TOP_K)
    flat_w = probs.reshape(-1)
    order = jnp.argsort(flat_e)
    se = flat_e[order]
    counts = jnp.bincount(flat_e, length=N_EXPERTS)
    padded = (counts + MOE_BLK - 1) // MOE_BLK * MOE_BLK
    start = jnp.cumsum(counts) - counts
    pend = jnp.cumsum(padded)
    pstart = pend - padded
    dest = pstart[se] + jnp.arange(nk) - start[se]
    n_rows = -(-nk // MOE_BLK) * MOE_BLK + N_EXPERTS * MOE_BLK
    n_blocks = n_rows // MOE_BLK
    tok = jnp.full((n_rows,), n, jnp.int32).at[dest].set(flat_t[order])
    wt = jnp.zeros((n_rows,), jnp.float32).at[dest].set(flat_w[order])
    blk_e = jnp.clip(jnp.searchsorted(pend, jnp.arange(n_blocks) * MOE_BLK, side='right'), 0, N_EXPERTS - 1)
    xb = jnp.concatenate([h, jnp.zeros((1, d), h.dtype)], axis=0)[tok].reshape(n_blocks, MOE_BLK, d)

    def expert_block(args):
        xe, e = args
        gu = xe @ w_gu[e] + b_gu[e]
        gate = jnp.minimum(gu[:, :D_FF], SWIGLU_LIMIT)
        up = jnp.clip(gu[:, D_FF:], -SWIGLU_LIMIT, SWIGLU_LIMIT)
        act = (up + 1.0) * gate * jax.nn.sigmoid(SWIGLU_ALPHA * gate)
        return act @ w_down[e] + b_down[e]

    yb = lax.map(expert_block, (xb, blk_e)).reshape(n_rows, d)
    y = jnp.zeros((n + 1, d), jnp.float32).at[tok].add(yb.astype(jnp.float32) * wt[:, None])
    return y[:n].astype(h.dtype)


def layer_forward(x, c, pos, attend, w_ada, b_ada, g_attn, w_in, g_qa, w_q_b, g_kva, g_qn, g_qr, g_kr,
                  w_o_mla, w_o_sb, w_out, g_ffn, w_router, b_router, w_gu, b_gu, w_down, b_down):
    sh1, sc1, gt1, sh2, sc2, gt2 = jnp.split(jax.nn.silu(c) @ w_ada + b_ada, 6, axis=-1)
    a = modulate(x, g_attn, sh1, sc1)
    q_m, q_s, ckv, krope, k_s, v_s, ga, gb = project(a, pos, w_in, g_qa, w_q_b, g_kva, g_qn, g_qr, g_kr)
    y_m, y_s = attend(q_m, q_s, ckv, krope, k_s, v_s)
    dt = x.dtype
    mixed = (ga * (y_m.astype(dt) @ w_o_mla) + gb * (y_s.astype(dt) @ w_o_sb)) @ w_out
    x = x + gt1[:, None, :] * mixed
    f = modulate(x, g_ffn, sh2, sc2)
    ff = moe(f.reshape(-1, D_MODEL), w_router, b_router, w_gu, b_gu, w_down, b_down).reshape(x.shape)
    x = x + gt2[:, None, :] * ff
    return x, (ckv, krope, k_s, v_s)


def setup_inputs(seed: int = 0) -> dict:
    key = jax.random.key(seed)
    keys = jax.random.split(key, 32)
    n_pages = PAST_LEN // PAGE_SIZE
    pool_pages = (DEC_BATCH * n_pages * 5) // 4
    L = DEPTH

    def nrm(i, shape, scale=1.0):
        return jax.random.normal(keys[i], shape, jnp.float32) * scale

    def gain(i, n):
        return 1.0 + nrm(i, (L, n), 0.05)

    page_table = jax.random.permutation(keys[6], pool_pages)[: DEC_BATCH * n_pages]
    page_table = page_table.reshape(DEC_BATCH, n_pages).astype(jnp.int32)
    return {
        'x_prompt': nrm(0, (BATCH, SEQ, D_MODEL)),
        'x_sample': nrm(1, (DEC_BATCH, DEC_SEQ, D_MODEL)),
        'cache_mla_ckv': nrm(2, (L, pool_pages, PAGE_SIZE, KV_RANK)),
        'cache_mla_krope': nrm(3, (L, pool_pages, PAGE_SIZE, ROPE_DIM)),
        'cache_sb_k': nrm(4, (L, pool_pages, PAGE_SIZE, SB_HEADS, SB_DIM)),
        'cache_sb_v': nrm(5, (L, pool_pages, PAGE_SIZE, SB_HEADS, SB_DIM)),
        'page_table': page_table,
        'c_prompt': nrm(7, (BATCH, D_MODEL)),
        'c_sample': nrm(8, (DEC_BATCH, D_MODEL)),
        'w_ada': nrm(9, (L, D_MODEL, 6 * D_MODEL), 0.5 * D_MODEL ** -0.5),
        'b_ada': nrm(10, (L, 6 * D_MODEL), 0.02),
        'g_attn': gain(11, D_MODEL),
        'w_in': nrm(12, (L, D_MODEL, D_IN), D_MODEL ** -0.5),
        'g_qa': gain(13, Q_RANK),
        'w_q_b': nrm(14, (L, Q_RANK, MLA_HEADS * MLA_QK_DIM), Q_RANK ** -0.5),
        'g_kva': gain(15, KV_RANK),
        'w_kv_b': nrm(16, (L, KV_RANK, MLA_HEADS * (NOPE_DIM + V_DIM)), KV_RANK ** -0.5),
        'g_qn': gain(17, NOPE_DIM),
        'g_qr': gain(18, ROPE_DIM),
        'g_kn': gain(19, NOPE_DIM),
        'g_kr': gain(20, ROPE_DIM),
        'w_o_mla': nrm(21, (L, MLA_HEADS * V_DIM, D_MODEL), (MLA_HEADS * V_DIM) ** -0.5),
        'w_o_sb': nrm(22, (L, SB_WIDTH, D_MODEL), SB_WIDTH ** -0.5),
        'w_out': nrm(23, (L, D_MODEL, D_MODEL), D_MODEL ** -0.5),
        'g_ffn': gain(24, D_MODEL),
        'w_router': nrm(25, (L, D_MODEL, N_EXPERTS), D_MODEL ** -0.5),
        'b_router': nrm(26, (L, N_EXPERTS), 0.01),
        'w_gu': nrm(27, (L, N_EXPERTS, D_MODEL, 2 * D_FF), D_MODEL ** -0.5),
        'b_gu': nrm(28, (L, N_EXPERTS, 2 * D_FF), 0.02),
        'w_down': nrm(29, (L, N_EXPERTS, D_FF, D_MODEL), D_FF ** -0.5),
        'b_down': nrm(30, (L, N_EXPERTS, D_MODEL), 0.02),
    }


def reference(x_prompt, x_sample, cache_mla_ckv, cache_mla_krope, cache_sb_k, cache_sb_v, page_table,
              c_prompt, c_sample, w_ada, b_ada, g_attn, w_in, g_qa, w_q_b, g_kva, w_kv_b, g_qn, g_qr,
              g_kn, g_kr, w_o_mla, w_o_sb, w_out, g_ffn, w_router, b_router, w_gu, b_gu, w_down, b_down):
    pos_p = jnp.arange(SEQ)
    pos_s = PAST_LEN + jnp.arange(DEC_SEQ)
    xp, xs = x_prompt, x_sample
    rows_p, rows_s = [], []
    for l in range(DEPTH):
        lw = (w_ada[l], b_ada[l], g_attn[l], w_in[l], g_qa[l], w_q_b[l], g_kva[l], g_qn[l], g_qr[l],
              g_kr[l], w_o_mla[l], w_o_sb[l], w_out[l], g_ffn[l], w_router[l], b_router[l], w_gu[l],
              b_gu[l], w_down[l], b_down[l])
        attend_p = functools.partial(prompt_attention, w_kv_b=w_kv_b[l], g_kn=g_kn[l])
        attend_s = functools.partial(sample_attention, cache_ckv=cache_mla_ckv, cache_krope=cache_mla_krope,
                                     cache_k=cache_sb_k, cache_v=cache_sb_v, page_table=page_table,
                                     layer=l, w_kv_b=w_kv_b[l], g_kn=g_kn[l])
        xp, rp = layer_forward(xp, c_prompt, pos_p, attend_p, *lw)
        xs, rs = layer_forward(xs, c_sample, pos_s, attend_s, *lw)
        rows_p.append(rp)
        rows_s.append(rs)
    new_ckv_p = jnp.stack([r[0] for r in rows_p])
    new_krope_p = jnp.stack([r[1] for r in rows_p])
    new_sbk_p = jnp.stack([r[2] for r in rows_p])
    new_sbv_p = jnp.stack([r[3] for r in rows_p])
    new_ckv_s = jnp.stack([r[0] for r in rows_s])
    new_krope_s = jnp.stack([r[1] for r in rows_s])
    new_sbk_s = jnp.stack([r[2] for r in rows_s])
    new_sbv_s = jnp.stack([r[3] for r in rows_s])
    return (xp, xs, new_ckv_p, new_krope_p, new_sbk_p, new_sbv_p, new_ckv_s, new_krope_s, new_sbk_s, new_sbv_s)
```

```python
import functools

import numpy as np
import jax
import jax.numpy as jnp
from jax import lax
from jax.experimental import pallas as pl
from jax.experimental.pallas import tpu as pltpu

F32 = jnp.float32
BF16 = jnp.bfloat16

MLA_HEADS = 8
Q_RANK = 256
KV_RANK = 128
NOPE_DIM = 64
ROPE_DIM = 32
V_DIM = 64
MLA_QK_DIM = NOPE_DIM + ROPE_DIM
MLA_SCALE = MLA_QK_DIM ** -0.5
ROPE_THETA = 10000.0
SB_HEADS = 8
SB_DIM = 64
SB_WIDTH = SB_HEADS * SB_DIM
SB_SCALE = SB_DIM ** -0.5
N_EXPERTS = 32
TOP_K = 4
SWIGLU_LIMIT = 7.0
SWIGLU_ALPHA = 1.702
EPS = 1e-6
NEG = -1e30

LANES = 128
HEAD_PAD = 128
VMEM_LIMIT = 56 * 1024 * 1024

ROW_TILE = 256
ATT_BLK = 256
PAGES_PER_STEP = 8
MOE_ROWS = 256

C_QA = 0
C_CKV = C_QA + Q_RANK
C_KR = C_CKV + KV_RANK
C_KRSW = C_KR + LANES
C_QS = C_KRSW + LANES
C_KS = C_QS + SB_WIDTH
C_VS = C_KS + SB_WIDTH
C_GATE = C_VS + SB_WIDTH


def _nt(a, b):
    return lax.dot_general(a, b, (((1,), (1,)), ((), ())), preferred_element_type=F32)


def _mm(a, b):
    return jnp.dot(a, b, preferred_element_type=F32)


def _split_bf16(x):
    hi = x.astype(BF16)
    lo = (x - hi.astype(F32)).astype(BF16)
    return hi, lo


def _rms(x):
    return lax.rsqrt(jnp.mean(x * x, axis=-1, keepdims=True) + EPS)


def _cparams(n_axes):
    return pltpu.CompilerParams(dimension_semantics=("arbitrary",) * n_axes,
                                vmem_limit_bytes=VMEM_LIMIT)


def _ada_kernel(c_ref, w_ref, b_ref, o_ref):
    c = c_ref[...]
    s = c * jax.nn.sigmoid(c)
    o_ref[...] = _mm(s.astype(BF16), w_ref[...]) + b_ref[...]


def _ada_call(c, w, b):
    n, d = c.shape
    cols = w.shape[1]
    return pl.pallas_call(
        _ada_kernel,
        grid=(cols // d,),
        in_specs=[pl.BlockSpec((n, d), lambda j: (0, 0)),
                  pl.BlockSpec((d, d), lambda j: (0, j)),
                  pl.BlockSpec((1, d), lambda j: (0, j))],
        out_specs=pl.BlockSpec((n, d), lambda j: (0, j)),
        out_shape=jax.ShapeDtypeStruct((n, cols), F32),
        compiler_params=_cparams(1),
        name="adaln",
    )(c, w, b)


def _proj_kernel(x_ref, sh_ref, sc_ref, cos_ref, sin_ref, gattn_ref, win_ref, gqa_ref, wqb_ref,
                 gq_ref, gqsw_ref, pq_ref, gkva_ref, wkb_ref, gk_ref, pk_ref, gkr_ref, gkrsw_ref,
                 ckv_o, krope_o, sbk_o, sbv_o, qm_o, km_o, ckvb_o, qs_o, ks_o, vs_o, gates_o):
    x = x_ref[...]
    a = x * _rms(x) * gattn_ref[...] * (1.0 + sc_ref[0]) + sh_ref[0]
    h = _mm(a.astype(BF16), win_ref[...])
    cosb = cos_ref[...]
    sinb = sin_ref[...]

    qa = h[:, C_QA:C_QA + Q_RANK]
    qa_n = qa * _rms(qa) * gqa_ref[...]
    qq = _mm(qa_n.astype(BF16), wqb_ref[...])
    gq = gq_ref[...]
    gqsw = gqsw_ref[...]
    pq = pq_ref[...]
    width = MLA_HEADS * HEAD_PAD
    for hd in range(MLA_HEADS):
        lo_c = hd * HEAD_PAD
        q = qq[:, lo_c:lo_c + HEAD_PAD]
        qsw = qq[:, width + lo_c:width + lo_c + HEAD_PAD]
        hi, lo = _split_bf16(q * q)
        rin = lax.rsqrt(_mm(hi, pq) + _mm(lo, pq) + EPS)
        qm_o[:, lo_c:lo_c + HEAD_PAD] = ((q * gq * cosb + qsw * gqsw * sinb) * rin).astype(qm_o.dtype)

    ckv_raw = h[:, C_CKV:C_CKV + KV_RANK]
    ckv = ckv_raw * _rms(ckv_raw) * gkva_ref[...]
    ckv_o[...] = ckv
    ckvb = ckv.astype(BF16)
    ckvb_o[...] = ckvb
    kr = h[:, C_KR:C_KR + LANES]
    krsw = h[:, C_KRSW:C_KRSW + LANES]
    kr_rin = lax.rsqrt(jnp.sum(kr * kr, axis=-1, keepdims=True) * (1.0 / ROPE_DIM) + EPS)
    krope = (kr * gkr_ref[...] * cosb + krsw * gkrsw_ref[...] * sinb) * kr_rin
    krope_o[...] = krope[:, NOPE_DIM:NOPE_DIM + ROPE_DIM]

    kexp = _mm(ckvb, wkb_ref[...])
    gk = gk_ref[...]
    pk = pk_ref[...]
    for hd in range(MLA_HEADS):
        lo_c = hd * HEAD_PAD
        k = kexp[:, lo_c:lo_c + HEAD_PAD]
        hi, lo = _split_bf16(k * k)
        rin = lax.rsqrt(_mm(hi, pk) + _mm(lo, pk) + EPS)
        km_o[:, lo_c:lo_c + HEAD_PAD] = (k * gk * rin + krope).astype(km_o.dtype)

    qs_o[...] = (h[:, C_QS:C_QS + SB_WIDTH] * SB_SCALE).astype(qs_o.dtype)
    ks = h[:, C_KS:C_KS + SB_WIDTH]
    vs = h[:, C_VS:C_VS + SB_WIDTH]
    sbk_o[...] = ks
    sbv_o[...] = vs
    ks_o[...] = ks.astype(BF16)
    vs_o[...] = vs.astype(BF16)
    gates_o[...] = jax.nn.sigmoid(h[:, C_GATE:]).astype(BF16)


def _row_specs(rows, tm, per_seq_rows):
    if per_seq_rows is not None:
        tps = per_seq_rows // tm
        mod = lambda d: pl.BlockSpec((1, 1, d), lambda i: (i // tps, 0, 0))
        tab = pl.BlockSpec((tm, LANES), lambda i: (i % tps, 0))
    else:
        mod = lambda d: pl.BlockSpec((1, tm, d), lambda i: (i, 0, 0))
        tab = pl.BlockSpec((tm, LANES), lambda i: (i, 0))
    return mod, tab


def _proj_call(x, sh, sc, cosb, sinb, w, per_seq_rows, q_dtype):
    rows, d = x.shape
    tm = min(ROW_TILE, rows)
    mod, tab = _row_specs(rows, tm, per_seq_rows)
    full = lambda arr: pl.BlockSpec(arr.shape, lambda i: (0,) * arr.ndim)
    rowblk = lambda c: pl.BlockSpec((tm, c), lambda i: (i, 0))
    hw = MLA_HEADS * HEAD_PAD
    consts = [w["g_attn"], w["w_in"], w["g_qa"], w["w_qb"], w["gq"], w["gqsw"], w["pq"], w["g_kva"],
              w["w_kb"], w["gk"], w["pk"], w["gkr"], w["gkrsw"]]
    out_cols = [(KV_RANK, F32), (ROPE_DIM, F32), (SB_WIDTH, F32), (SB_WIDTH, F32), (hw, q_dtype),
                (hw, q_dtype), (KV_RANK, BF16), (SB_WIDTH, q_dtype), (SB_WIDTH, BF16), (SB_WIDTH, BF16),
                (2 * d, BF16)]
    return pl.pallas_call(
        _proj_kernel,
        grid=(rows // tm,),
        in_specs=[rowblk(d), mod(d), mod(d), tab, tab] + [full(c) for c in consts],
        out_specs=[rowblk(c) for c, _ in out_cols],
        out_shape=[jax.ShapeDtypeStruct((rows, c), dt) for c, dt in out_cols],
        compiler_params=_cparams(1),
        name="in_proj",
    )(x, sh, sc, cosb, sinb, *consts)


def _log1m(z):
    return jnp.minimum(-z, 0.0) - jnp.log(1.0 + jnp.exp(-jnp.abs(z)))


def _suffix_sums(log1m, tri):
    hi, lo = _split_bf16(log1m)
    return _mm(hi, tri) + _mm(lo, tri)


def _prompt_attn_kernel(qm_ref, km_ref, ckv_ref, qs_ref, ks_ref, vs_ref, wv_ref, tri_ref,
                        ym_ref, ys_ref, m_sc, l_sc, om_sc, ls_sc, os_sc, *, blk):
    qi = pl.program_id(2)
    lane = lax.broadcasted_iota(jnp.int32, (blk, LANES), 1)
    tri = tri_ref[...]
    qs_pair = qs_ref[0]
    zero = jnp.zeros_like(qs_pair)
    half = LANES // 2
    q_s = [jnp.where(lane < half, qs_pair, zero), jnp.where(lane >= half, qs_pair, zero)]
    q_m = [qm_ref[0, :, 0:HEAD_PAD], qm_ref[0, :, HEAD_PAD:2 * HEAD_PAD]]

    def block(j, diag):
        k0 = pl.multiple_of(j * blk, blk)
        ckv = ckv_ref[0, pl.ds(k0, blk), :]
        ks = ks_ref[0, pl.ds(k0, blk), :]
        vs = vs_ref[0, pl.ds(k0, blk), :]
        if diag:
            row = lax.broadcasted_iota(jnp.int32, (blk, blk), 0)
            col = lax.broadcasted_iota(jnp.int32, (blk, blk), 1)
            causal = col <= row
            strict = col < row
        for hh in range(2):
            km = km_ref[0, pl.ds(k0, blk), hh * HEAD_PAD:(hh + 1) * HEAD_PAD]
            s = _nt(q_m[hh], km)
            if diag:
                s = jnp.where(causal, s, NEG)
                m_new = jnp.max(s, axis=-1, keepdims=True)
                p = jnp.where(causal, jnp.exp(s - m_new), 0.0)
                l_sc[hh] = jnp.sum(p, axis=-1, keepdims=True)
                om_sc[hh] = _mm(p.astype(BF16), ckv)
            else:
                m_prev = m_sc[hh]
                m_new = jnp.maximum(m_prev, jnp.max(s, axis=-1, keepdims=True))
                alpha = jnp.exp(m_prev - m_new)
                p = jnp.exp(s - m_new)
                l_sc[hh] = alpha * l_sc[hh] + jnp.sum(p, axis=-1, keepdims=True)
                om_sc[hh] = alpha * om_sc[hh] + _mm(p.astype(BF16), ckv)
            m_sc[hh] = m_new

            z = _nt(q_s[hh], ks)
            log1m = _log1m(z)
            if diag:
                log1m = jnp.where(strict, log1m, 0.0)
            c = _suffix_sums(log1m, tri)
            if diag:
                a = jnp.where(strict, jnp.exp(z + c), 0.0)
                os_sc[hh] = _mm(a.astype(BF16), vs)
                ls_sc[hh] = c[:, 0:1]
            else:
                ls = ls_sc[hh]
                a = jnp.exp(z + c + ls)
                os_sc[hh] = os_sc[hh] + _mm(a.astype(BF16), vs)
                ls_sc[hh] = ls + c[:, 0:1]

    block(qi, True)

    def body(it, carry):
        block(qi - 1 - it, False)
        return carry

    lax.fori_loop(0, qi, body, 0)

    ylat = [(om_sc[hh] / l_sc[hh]).astype(BF16) for hh in range(2)]
    ym_ref[0] = _mm(jnp.concatenate(ylat, axis=1), wv_ref[0]).astype(BF16)
    ys_ref[0] = jnp.where(lane < half, os_sc[0], os_sc[1]).astype(BF16)


def _prompt_attn_call(qm, km, ckvb, qs, ks, vs, wv_pair, tri, blk):
    b, s, _ = qm.shape
    hp = MLA_HEADS // 2
    qspec = lambda c: pl.BlockSpec((1, blk, c), lambda bi, h, qi: (bi, qi, h))
    kspec = lambda c: pl.BlockSpec((1, s, c), lambda bi, h, qi: (bi, 0, h))
    return pl.pallas_call(
        functools.partial(_prompt_attn_kernel, blk=blk),
        grid=(b, hp, s // blk),
        in_specs=[qspec(2 * HEAD_PAD), kspec(2 * HEAD_PAD),
                  pl.BlockSpec((1, s, KV_RANK), lambda bi, h, qi: (bi, 0, 0)),
                  qspec(LANES), kspec(LANES), kspec(LANES),
                  pl.BlockSpec((1, 2 * KV_RANK, LANES), lambda bi, h, qi: (h, 0, 0)),
                  pl.BlockSpec((blk, blk), lambda bi, h, qi: (0, 0))],
        out_specs=[qspec(LANES), qspec(LANES)],
        out_shape=[jax.ShapeDtypeStruct((b, s, MLA_HEADS * V_DIM), BF16),
                   jax.ShapeDtypeStruct((b, s, SB_WIDTH), BF16)],
        scratch_shapes=[pltpu.VMEM((2, blk, 1), F32), pltpu.VMEM((2, blk, 1), F32),
                        pltpu.VMEM((2, blk, KV_RANK), F32), pltpu.VMEM((2, blk, 1), F32),
                        pltpu.VMEM((2, blk, LANES), F32)],
        compiler_params=_cparams(3),
        name="prompt_attn",
    )(qm, km, ckvb, qs, ks, vs, wv_pair, tri)


def _sample_attn_kernel(pt_ref, qm_ref, kmn_ref, ckvn_ref, qs_ref, ksn_ref, vsn_ref,
                        wabs_ref, wkb_ref, wv_ref, emean_ref, tri_ref, *rest, t_new, page, n_grp):
    n_pg = PAGES_PER_STEP
    ckv_refs = rest[0:n_pg]
    kr_refs = rest[n_pg:2 * n_pg]
    k_refs = rest[2 * n_pg:3 * n_pg]
    v_refs = rest[3 * n_pg:4 * n_pg]
    ym_ref, ys_ref = rest[4 * n_pg:4 * n_pg + 2]
    qabs_sc, qr_sc, qbd_sc, m_sc, l_sc, om_sc, ls_sc, os_sc = rest[4 * n_pg + 2:]

    p = pl.program_id(1)
    rows = MLA_HEADS * t_new
    hw = MLA_HEADS * HEAD_PAD
    tri = tri_ref[...]

    def head_mask(cols, per_head):
        r = lax.broadcasted_iota(jnp.int32, (rows, cols), 0) // t_new
        c = lax.broadcasted_iota(jnp.int32, (rows, cols), 1) // per_head
        return r == c

    @pl.when(p == 0)
    def _init():
        pad = page - t_new
        padk = lambda ref: jnp.concatenate(
            [ref[...], jnp.zeros((pad, ref.shape[1]), F32)], axis=0).astype(BF16)
        qm = jnp.concatenate([qm_ref[...]] * MLA_HEADS, axis=0)
        qbd_m = jnp.where(head_mask(hw, HEAD_PAD), qm, 0.0).astype(BF16)
        qabs_sc[...] = _mm(qbd_m, wabs_ref[...]).astype(BF16)
        qr = [qm_ref[:, hd * HEAD_PAD + NOPE_DIM:hd * HEAD_PAD + NOPE_DIM + ROPE_DIM]
              for hd in range(MLA_HEADS)]
        qr_sc[...] = jnp.concatenate(qr, axis=0).astype(BF16)
        qs = jnp.concatenate([qs_ref[...]] * SB_HEADS, axis=0)
        qbd_s = jnp.where(head_mask(SB_WIDTH, SB_DIM), qs, 0.0).astype(BF16)
        qbd_sc[...] = qbd_s

        qrow = lax.broadcasted_iota(jnp.int32, (rows, page), 0) % t_new
        kcol = lax.broadcasted_iota(jnp.int32, (rows, page), 1)
        causal = kcol <= qrow
        strict = kcol < qrow
        s = jnp.where(causal, _nt(qbd_m, padk(kmn_ref)), NEG)
        m_new = jnp.max(s, axis=-1, keepdims=True)
        pr = jnp.where(causal, jnp.exp(s - m_new), 0.0)
        m_sc[...] = m_new
        l_sc[...] = jnp.sum(pr, axis=-1, keepdims=True)
        om_sc[...] = _mm(pr.astype(BF16), padk(ckvn_ref))
        z = _nt(qbd_s, padk(ksn_ref))
        log1m = jnp.where(strict, _log1m(z), 0.0)
        c = _suffix_sums(log1m, tri[0:page, 0:page])
        a = jnp.where(strict, jnp.exp(z + c), 0.0)
        os_sc[...] = _mm(a.astype(BF16), padk(vsn_ref))
        ls_sc[...] = c[:, 0:1]

    qabs = qabs_sc[...]
    qr = qr_sc[...]
    qbd_s = qbd_sc[...]
    wkb = wkb_ref[...]
    emean = emean_ref[...]
    for cidx in range(n_pg // 2):
        newer, older = 2 * cidx, 2 * cidx + 1
        s_parts, z_parts, ckvs, vss = [], [], [], []
        for j in (older, newer):
            ckv = ckv_refs[j][0, 0].astype(BF16)
            kexp = _mm(ckv, wkb)
            ss = _nt(emean, (kexp * kexp).astype(BF16))
            s_j = _nt(qabs, ckv) * lax.rsqrt(ss + EPS) + _nt(qr, kr_refs[j][0, 0].astype(BF16))
            s_parts.append(s_j)
            z_parts.append(_nt(qbd_s, k_refs[j][0, 0].astype(BF16)))
            ckvs.append(ckv)
            vss.append(v_refs[j][0, 0].astype(BF16))
        s = jnp.concatenate(s_parts, axis=1)
        m_prev = m_sc[...]
        m_new = jnp.maximum(m_prev, jnp.max(s, axis=-1, keepdims=True))
        alpha = jnp.exp(m_prev - m_new)
        pr = jnp.exp(s - m_new).astype(BF16)
        l_sc[...] = alpha * l_sc[...] + jnp.sum(pr.astype(F32), axis=-1, keepdims=True)
        om_sc[...] = (alpha * om_sc[...] + _mm(pr[:, 0:page], ckvs[0]) + _mm(pr[:, page:], ckvs[1]))
        m_sc[...] = m_new

        z = jnp.concatenate(z_parts, axis=1)
        c = _suffix_sums(_log1m(z), tri)
        ls = ls_sc[...]
        a = jnp.exp(z + c + ls).astype(BF16)
        os_sc[...] = os_sc[...] + _mm(a[:, 0:page], vss[0]) + _mm(a[:, page:], vss[1])
        ls_sc[...] = ls + c[:, 0:1]

    @pl.when(p == n_grp - 1)
    def _fin():
        ylat = (om_sc[...] / l_sc[...]).astype(BF16)
        full_m = jnp.where(head_mask(MLA_HEADS * V_DIM, V_DIM), _mm(ylat, wv_ref[...]), 0.0)
        full_s = jnp.where(head_mask(SB_WIDTH, SB_DIM), os_sc[...], 0.0)
        ym = full_m[0:t_new]
        ys = full_s[0:t_new]
        for hd in range(1, MLA_HEADS):
            ym = ym + full_m[hd * t_new:(hd + 1) * t_new]
            ys = ys + full_s[hd * t_new:(hd + 1) * t_new]
        ym_ref[...] = ym.astype(BF16)
        ys_ref[...] = ys.astype(BF16)


def _sample_attn_call(page_table, qm, kmn, ckvn, qs, ksn, vsn, w, tri, caches, layer, t_new):
    cache_ckv, cache_kr, cache_k, cache_v = caches
    nseq, n_pages = page_table.shape
    page = cache_ckv.shape[2]
    n_pg = PAGES_PER_STEP
    n_grp = n_pages // n_pg
    rows = MLA_HEADS * t_new
    pt = page_table.reshape(-1)

    seqblk = lambda c: pl.BlockSpec((t_new, c), lambda b, p, pt_r: (b, 0))
    full = lambda arr: pl.BlockSpec(arr.shape, lambda b, p, pt_r: (0,) * arr.ndim)

    def page_spec(j, c):
        def imap(b, p, pt_r):
            return (layer, pt_r[b * n_pages + (n_pages - 1 - (p * n_pg + j))], 0, 0)
        return pl.BlockSpec((1, 1, page, c), imap)

    consts = [w["wabs"], w["wkb_s"], w["wv_s"], w["emean"], tri]
    page_specs = ([page_spec(j, KV_RANK) for j in range(n_pg)]
                  + [page_spec(j, ROPE_DIM) for j in range(n_pg)]
                  + [page_spec(j, SB_WIDTH) for j in range(n_pg)]
                  + [page_spec(j, SB_WIDTH) for j in range(n_pg)])
    page_args = [cache_ckv] * n_pg + [cache_kr] * n_pg + [cache_k] * n_pg + [cache_v] * n_pg
    hw = MLA_HEADS * HEAD_PAD
    grid_spec = pltpu.PrefetchScalarGridSpec(
        num_scalar_prefetch=1,
        grid=(nseq, n_grp),
        in_specs=[seqblk(hw), seqblk(hw), seqblk(KV_RANK), seqblk(SB_WIDTH), seqblk(SB_WIDTH),
                  seqblk(SB_WIDTH)] + [full(c) for c in consts] + page_specs,
        out_specs=[seqblk(MLA_HEADS * V_DIM), seqblk(SB_WIDTH)],
        scratch_shapes=[pltpu.VMEM((rows, KV_RANK), BF16), pltpu.VMEM((rows, ROPE_DIM), BF16),
                        pltpu.VMEM((rows, SB_WIDTH), BF16),
                        pltpu.VMEM((rows, 1), F32), pltpu.VMEM((rows, 1), F32),
                        pltpu.VMEM((rows, KV_RANK), F32), pltpu.VMEM((rows, 1), F32),
                        pltpu.VMEM((rows, SB_WIDTH), F32)],
    )
    return pl.pallas_call(
        functools.partial(_sample_attn_kernel, t_new=t_new, page=page, n_grp=n_grp),
        grid_spec=grid_spec,
        out_shape=[jax.ShapeDtypeStruct((nseq * t_new, MLA_HEADS * V_DIM), BF16),
                   jax.ShapeDtypeStruct((nseq * t_new, SB_WIDTH), BF16)],
        compiler_params=_cparams(2),
        name="sample_attn",
    )(pt, qm, kmn, ckvn, qs, ksn, vsn, *consts, *page_args)


def _out_kernel(x_ref, ym_ref, ys_ref, gates_ref, gt1_ref, sh2_ref, sc2_ref, woa_ref, wob_ref, wout_ref,
                gffn_ref, wr_ref, br_ref, x1_o, f_o, topi_o, prob_o):
    d = x_ref.shape[1]
    ga = gates_ref[:, 0:d].astype(F32)
    gb = gates_ref[:, d:2 * d].astype(F32)
    mix = ga * _mm(ym_ref[...], woa_ref[...]) + gb * _mm(ys_ref[...], wob_ref[...])
    x1 = x_ref[...] + gt1_ref[0] * _mm(mix.astype(BF16), wout_ref[...])
    x1_o[...] = x1
    f = x1 * _rms(x1) * gffn_ref[...] * (1.0 + sc2_ref[0]) + sh2_ref[0]
    f_o[...] = f.astype(BF16)

    logits = jnp.dot(f, wr_ref[...], preferred_element_type=F32,
                     precision=lax.Precision.HIGHEST) + br_ref[...]
    lane = lax.broadcasted_iota(jnp.int32, logits.shape, 1)
    lanef = lane.astype(F32)
    neg_inf = jnp.float32(-jnp.inf)
    work = jnp.where(lane < N_EXPERTS, logits, neg_inf)
    topv = []
    topi = jnp.zeros(logits.shape, F32)
    for k in range(TOP_K):
        mx = jnp.max(work, axis=-1, keepdims=True)
        idx = jnp.min(jnp.where(work == mx, lanef, float(LANES)), axis=-1, keepdims=True)
        topv.append(mx)
        topi = jnp.where(lane == k, idx, topi)
        work = jnp.where(lanef == idx, neg_inf, work)
    ex = [jnp.exp(v - topv[0]) for v in topv]
    den = ex[0]
    for e in ex[1:]:
        den = den + e
    prob = jnp.zeros(logits.shape, F32)
    for k in range(TOP_K):
        prob = jnp.where(lane == k, ex[k] / den, prob)
    topi_o[...] = topi.astype(jnp.int32)
    prob_o[...] = prob


def _out_call(x, ym, ys, gates, gt1, sh2, sc2, w, per_seq_rows):
    rows, d = x.shape
    tm = min(ROW_TILE, rows)
    mod, _ = _row_specs(rows, tm, per_seq_rows)
    full = lambda arr: pl.BlockSpec(arr.shape, lambda i: (0,) * arr.ndim)
    rowblk = lambda c: pl.BlockSpec((tm, c), lambda i: (i, 0))
    consts = [w["w_o_mla"], w["w_o_sb"], w["w_out"], w["g_ffn"], w["w_router"], w["b_router"]]
    return pl.pallas_call(
        _out_kernel,
        grid=(rows // tm,),
        in_specs=[rowblk(d), rowblk(ym.shape[1]), rowblk(ys.shape[1]), rowblk(2 * d),
                  mod(d), mod(d), mod(d)] + [full(c) for c in consts],
        out_specs=[rowblk(d), rowblk(d), rowblk(LANES), rowblk(LANES)],
        out_shape=[jax.ShapeDtypeStruct((rows, d), F32), jax.ShapeDtypeStruct((rows, d), BF16),
                   jax.ShapeDtypeStruct((rows, LANES), jnp.int32),
                   jax.ShapeDtypeStruct((rows, LANES), F32)],
        compiler_params=_cparams(1),
        name="out_proj",
    )(x, ym, ys, gates, gt1, sh2, sc2, *consts)


def _moe_kernel(be_ref, nu_ref, x_ref, wgu_ref, bgu_ref, wd_ref, bd_ref, o_ref):
    i = pl.program_id(0)
    d_ff = wd_ref.shape[1]

    @pl.when(i < nu_ref[0])
    def _compute():
        gu = _mm(x_ref[...], wgu_ref[0]) + bgu_ref[0]
        gate = jnp.minimum(gu[:, :d_ff], SWIGLU_LIMIT)
        up = jnp.clip(gu[:, d_ff:], -SWIGLU_LIMIT, SWIGLU_LIMIT)
        act = (up + 1.0) * gate * jax.nn.sigmoid(SWIGLU_ALPHA * gate)
        o_ref[...] = (_mm(act.astype(BF16), wd_ref[0]) + bd_ref[0]).astype(o_ref.dtype)

    @pl.when(i >= nu_ref[0])
    def _unused():
        o_ref[...] = jnp.zeros(o_ref.shape, o_ref.dtype)


def _moe_call(blk_e, n_used, xb, w_gu, b_gu, w_down, b_down):
    n_rows, d = xb.shape
    mb = MOE_ROWS
    d_ff = w_down.shape[1]
    grid_spec = pltpu.PrefetchScalarGridSpec(
        num_scalar_prefetch=2,
        grid=(n_rows // mb,),
        in_specs=[pl.BlockSpec((mb, d), lambda i, be, nu: (i, 0)),
                  pl.BlockSpec((1, d, 2 * d_ff), lambda i, be, nu: (be[i], 0, 0)),
                  pl.BlockSpec((1, 1, 2 * d_ff), lambda i, be, nu: (be[i], 0, 0)),
                  pl.BlockSpec((1, d_ff, d), lambda i, be, nu: (be[i], 0, 0)),
                  pl.BlockSpec((1, 1, d), lambda i, be, nu: (be[i], 0, 0))],
        out_specs=pl.BlockSpec((mb, d), lambda i, be, nu: (i, 0)),
    )
    return pl.pallas_call(
        _moe_kernel,
        grid_spec=grid_spec,
        out_shape=jax.ShapeDtypeStruct((n_rows, d), BF16),
        compiler_params=_cparams(1),
        name="moe_mlp",
    )(blk_e, n_used, xb, w_gu, b_gu, w_down, b_down)


def _route(top_i, n, mb):
    nk = n * TOP_K
    flat_e = top_i.reshape(-1)
    order = jnp.argsort(flat_e)
    se = flat_e[order]
    counts = jnp.bincount(flat_e, length=N_EXPERTS)
    padded = (counts + mb - 1) // mb * mb
    start = jnp.cumsum(counts) - counts
    pend = jnp.cumsum(padded)
    pstart = pend - padded
    dest_sorted = (pstart[se] + jnp.arange(nk) - start[se]).astype(jnp.int32)
    n_blocks = -(-nk // mb) + N_EXPERTS
    n_rows = n_blocks * mb
    tok = jnp.full((n_rows,), n, jnp.int32).at[dest_sorted].set((order // TOP_K).astype(jnp.int32))
    dest = jnp.zeros((nk,), jnp.int32).at[order].set(dest_sorted)
    blk_e = jnp.clip(jnp.searchsorted(pend, jnp.arange(n_blocks) * mb, side="right"),
                     0, N_EXPERTS - 1).astype(jnp.int32)
    n_used = (pend[-1] // mb).astype(jnp.int32).reshape(1)
    return tok, dest, blk_e, n_used


def _combine_kernel(x1_ref, gt2_ref, prob_ref, yk_ref, o_ref):
    d = x1_ref.shape[1]
    prob = prob_ref[...]
    ff = prob[:, 0:1] * yk_ref[:, 0:d].astype(F32)
    for k in range(1, TOP_K):
        ff = ff + prob[:, k:k + 1] * yk_ref[:, k * d:(k + 1) * d].astype(F32)
    o_ref[...] = x1_ref[...] + gt2_ref[0] * ff


def _combine_call(x1, gt2, prob, yk, per_seq_rows):
    rows, d = x1.shape
    tm = min(ROW_TILE, rows)
    mod, _ = _row_specs(rows, tm, per_seq_rows)
    rowblk = lambda c: pl.BlockSpec((tm, c), lambda i: (i, 0))
    return pl.pallas_call(
        _combine_kernel,
        grid=(rows // tm,),
        in_specs=[rowblk(d), mod(d), rowblk(LANES), rowblk(TOP_K * d)],
        out_specs=rowblk(d),
        out_shape=jax.ShapeDtypeStruct((rows, d), F32),
        compiler_params=_cparams(1),
        name="combine",
    )(x1, gt2, prob, yk)


def _prep_weights(l, w_ada, b_ada, g_attn, w_in, g_qa, w_q_b, g_kva, w_kv_b, g_qn, g_qr, g_kn, g_kr,
                  w_o_mla, w_o_sb, w_out, g_ffn, w_router, b_router):
    d = w_in.shape[1]
    half = ROPE_DIM // 2
    z = lambda *s: jnp.zeros(s, F32)
    wi = w_in[l]
    o = 0
    cuts = []
    for width in (Q_RANK, KV_RANK, ROPE_DIM, SB_WIDTH, SB_WIDTH, SB_WIDTH, d, d):
        cuts.append(wi[:, o:o + width])
        o += width
    c_qa, c_ckv, c_kr, c_qs, c_ks, c_vs, c_ga, c_gb = cuts
    tail = LANES - NOPE_DIM - ROPE_DIM
    kr_blk = jnp.concatenate([z(d, NOPE_DIM), c_kr, z(d, tail)], axis=1)
    krsw_blk = jnp.concatenate([z(d, NOPE_DIM), c_kr[:, half:], c_kr[:, :half], z(d, tail)], axis=1)
    w_in2 = jnp.concatenate([c_qa, c_ckv, kr_blk, krsw_blk, c_qs, c_ks, c_vs, c_ga, c_gb], axis=1)

    wq = w_q_b[l].reshape(Q_RANK, MLA_HEADS, MLA_QK_DIM)
    wq_main = jnp.concatenate([wq, z(Q_RANK, MLA_HEADS, tail)], axis=-1)
    wq_sw = jnp.concatenate([z(Q_RANK, MLA_HEADS, NOPE_DIM), wq[:, :, NOPE_DIM + half:],
                             wq[:, :, NOPE_DIM:NOPE_DIM + half], z(Q_RANK, MLA_HEADS, tail)], axis=-1)
    w_qb = jnp.concatenate([wq_main.reshape(Q_RANK, -1), wq_sw.reshape(Q_RANK, -1)], axis=1)

    wkv = w_kv_b[l].reshape(KV_RANK, MLA_HEADS, NOPE_DIM + V_DIM)
    wk = wkv[:, :, :NOPE_DIM]
    wv = wkv[:, :, NOPE_DIM:]
    w_kb = jnp.concatenate([wk, z(KV_RANK, MLA_HEADS, HEAD_PAD - NOPE_DIM)], axis=-1).reshape(KV_RANK, -1)
    wv_t = wv.transpose(1, 0, 2)
    zb = z(MLA_HEADS // 2, KV_RANK, V_DIM)
    wv_pair = jnp.concatenate([jnp.concatenate([wv_t[0::2], zb], axis=2),
                               jnp.concatenate([zb, wv_t[1::2]], axis=2)], axis=1)
    wabs = jnp.concatenate([(wk * g_kn[l][None, None, :]).transpose(1, 2, 0),
                            z(MLA_HEADS, HEAD_PAD - NOPE_DIM, KV_RANK)], axis=1).reshape(-1, KV_RANK)

    lane = np.arange(LANES)
    grp_n = lane < NOPE_DIM
    grp_r = (lane >= NOPE_DIM) & (lane < MLA_QK_DIM)
    pq = (np.outer(grp_n, grp_n) / NOPE_DIM + np.outer(grp_r, grp_r) / ROPE_DIM).astype(np.float32)
    pk = (np.outer(grp_n, np.ones(LANES, bool)) / NOPE_DIM).astype(np.float32)
    rows = MLA_HEADS * 8
    tailz = z(tail)
    row1 = lambda v: v.reshape(1, -1)
    gqr, gkr_ = g_qr[l], g_kr[l]
    return {
        "w_ada": w_ada[l].astype(BF16), "b_ada": row1(b_ada[l]),
        "g_attn": row1(g_attn[l]), "w_in": w_in2.astype(BF16), "g_qa": row1(g_qa[l]),
        "w_qb": w_qb.astype(BF16),
        "gq": row1(jnp.concatenate([g_qn[l], gqr, tailz]) * MLA_SCALE),
        "gqsw": row1(jnp.concatenate([z(NOPE_DIM), gqr[half:], gqr[:half], tailz]) * MLA_SCALE),
        "pq": jnp.asarray(pq, BF16), "g_kva": row1(g_kva[l]), "w_kb": w_kb.astype(BF16),
        "gk": row1(jnp.concatenate([g_kn[l], z(HEAD_PAD - NOPE_DIM)])),
        "pk": jnp.asarray(pk, BF16),
        "gkr": row1(jnp.concatenate([z(NOPE_DIM), gkr_, tailz])),
        "gkrsw": row1(jnp.concatenate([z(NOPE_DIM), gkr_[half:], gkr_[:half], tailz])),
        "wv_pair": wv_pair.astype(BF16),
        "wabs": wabs.astype(BF16), "wkb_s": wk.reshape(KV_RANK, -1).astype(BF16),
        "wv_s": wv.reshape(KV_RANK, -1).astype(BF16),
        "w_o_mla": w_o_mla[l].astype(BF16), "w_o_sb": w_o_sb[l].astype(BF16),
        "w_out": w_out[l].astype(BF16), "g_ffn": row1(g_ffn[l]),
        "w_router": jnp.concatenate([w_router[l], z(d, LANES - N_EXPERTS)], axis=1),
        "b_router": row1(jnp.concatenate([b_router[l], z(LANES - N_EXPERTS)])),
    }


def _emean(t_new):
    rows = MLA_HEADS * t_new
    r = np.arange(rows)[:, None] // t_new
    c = np.arange(MLA_HEADS * NOPE_DIM)[None, :] // NOPE_DIM
    return jnp.asarray((r == c) / NOPE_DIM, BF16)


def _rope_tables(pos):
    half = ROPE_DIM // 2
    inv = ROPE_THETA ** (-jnp.arange(half, dtype=F32) / half)
    ang = pos.astype(F32)[:, None] * inv[None, :]
    cos, sin = jnp.cos(ang), jnp.sin(ang)
    n = pos.shape[0]
    tail = jnp.zeros((n, LANES - MLA_QK_DIM), F32)
    cosb = jnp.concatenate([jnp.ones((n, NOPE_DIM), F32), cos, cos, tail], axis=1)
    sinb = jnp.concatenate([jnp.zeros((n, NOPE_DIM), F32), -sin, sin, tail], axis=1)
    return cosb, sinb


def _tri(n):
    j = np.arange(n)
    return jnp.asarray(j[:, None] >= j[None, :], BF16)


def kernel(x_prompt, x_sample, cache_mla_ckv, cache_mla_krope, cache_sb_k, cache_sb_v, page_table,
           c_prompt, c_sample, w_ada, b_ada, g_attn, w_in, g_qa, w_q_b, g_kva, w_kv_b, g_qn, g_qr,
           g_kn, g_kr, w_o_mla, w_o_sb, w_out, g_ffn, w_router, b_router, w_gu, b_gu, w_down, b_down):
    bsz, seq, d = x_prompt.shape
    nseq, t_new, _ = x_sample.shape
    depth = w_in.shape[0]
    n_pages = page_table.shape[1]
    page = cache_mla_ckv.shape[2]
    past_len = n_pages * page
    n_p, n_s = bsz * seq, nseq * t_new
    blk = min(ATT_BLK, seq)

    cos_p, sin_p = _rope_tables(jnp.arange(seq))
    cos_s, sin_s = _rope_tables(past_len + jnp.arange(t_new))
    cos_s, sin_s = jnp.tile(cos_s, (nseq, 1)), jnp.tile(sin_s, (nseq, 1))
    tri = _tri(blk)
    tri_s = _tri(2 * page)
    emean = _emean(t_new)
    cache_k = cache_sb_k.reshape(cache_sb_k.shape[:3] + (SB_WIDTH,))
    cache_v = cache_sb_v.reshape(cache_sb_v.shape[:3] + (SB_WIDTH,))
    tm_s = min(ROW_TILE, n_s)

    xp = x_prompt.reshape(n_p, d)
    xs = x_sample.reshape(n_s, d)
    c_all = jnp.concatenate([c_prompt, c_sample], axis=0)
    rows_p, rows_s = [], []
    for l in range(depth):
        w = _prep_weights(l, w_ada, b_ada, g_attn, w_in, g_qa, w_q_b, g_kva, w_kv_b, g_qn, g_qr, g_kn,
                          g_kr, w_o_mla, w_o_sb, w_out, g_ffn, w_router, b_router)
        w["emean"] = emean
        mod = _ada_call(c_all, w["w_ada"], w["b_ada"])
        mods_p = [mod[:bsz, i * d:(i + 1) * d].reshape(bsz, 1, d) for i in range(6)]
        mods_s = [jnp.repeat(mod[bsz:, i * d:(i + 1) * d], t_new, axis=0).reshape(n_s // tm_s, tm_s, d)
                  for i in range(6)]

        (ckv_p, kr_p, sbk_p, sbv_p, qm, km, ckvb, qs, ks, vs, gates_p) = _proj_call(
            xp, mods_p[0], mods_p[1], cos_p, sin_p, w, seq, BF16)
        r3 = lambda a: a.reshape(bsz, seq, a.shape[1])
        ym_p, ys_p = _prompt_attn_call(r3(qm), r3(km), r3(ckvb), r3(qs), r3(ks), r3(vs),
                                       w["wv_pair"], tri, blk)
        x1_p, f_p, ti_p, pr_p = _out_call(xp, ym_p.reshape(n_p, -1), ys_p.reshape(n_p, -1), gates_p,
                                          mods_p[2], mods_p[3], mods_p[4], w, seq)

        (ckv_s, kr_s, sbk_s, sbv_s, qm_s, km_s, _, qs_s, _, _, gates_s) = _proj_call(
            xs, mods_s[0], mods_s[1], cos_s, sin_s, w, None, F32)
        ym_s, ys_s = _sample_attn_call(page_table, qm_s, km_s, ckv_s, qs_s, sbk_s, sbv_s, w, tri_s,
                                       (cache_mla_ckv, cache_mla_krope, cache_k, cache_v), l, t_new)
        x1_s, f_s, ti_s, pr_s = _out_call(xs, ym_s, ys_s, gates_s, mods_s[2], mods_s[3], mods_s[4], w, None)

        n = n_p + n_s
        f_all = jnp.concatenate([f_p, f_s, jnp.zeros((1, d), BF16)], axis=0)
        top_i = jnp.concatenate([ti_p[:, :TOP_K], ti_s[:, :TOP_K]], axis=0)
        tok, dest, blk_e, n_used = _route(top_i, n, MOE_ROWS)
        yb = _moe_call(blk_e, n_used, f_all[tok], w_gu[l].astype(BF16),
                       b_gu[l].reshape(N_EXPERTS, 1, -1), w_down[l].astype(BF16),
                       b_down[l].reshape(N_EXPERTS, 1, -1))
        yk = yb[dest].reshape(n, TOP_K * d)
        xp = _combine_call(x1_p, mods_p[5], pr_p, yk[:n_p], seq)
        xs = _combine_call(x1_s, mods_s[5], pr_s, yk[n_p:], None)

        rows_p.append((ckv_p, kr_p, sbk_p, sbv_p))
        rows_s.append((ckv_s, kr_s, sbk_s, sbv_s))

    stack_p = lambda i, tail: jnp.stack([r[i] for r in rows_p]).reshape((depth, bsz, seq) + tail)
    stack_s = lambda i, tail: jnp.stack([r[i] for r in rows_s]).reshape((depth, nseq, t_new) + tail)
    return (xp.reshape(bsz, seq, d), xs.reshape(nseq, t_new, d),
            stack_p(0, (KV_RANK,)), stack_p(1, (ROPE_DIM,)),
            stack_p(2, (SB_HEADS, SB_DIM)), stack_p(3, (SB_HEADS, SB_DIM)),
            stack_s(0, (KV_RANK,)), stack_s(1, (ROPE_DIM,)),
            stack_s(2, (SB_HEADS, SB_DIM)), stack_s(3, (SB_HEADS, SB_DIM)))
```

```python
import functools

import numpy as np
import jax
import jax.numpy as jnp
from jax import lax
from jax.experimental import pallas as pl
from jax.experimental.pallas import tpu as pltpu

F32 = jnp.float32
BF16 = jnp.bfloat16

MLA_HEADS = 8
Q_RANK = 256
KV_RANK = 128
NOPE_DIM = 64
ROPE_DIM = 32
V_DIM = 64
MLA_QK_DIM = NOPE_DIM + ROPE_DIM
MLA_SCALE = MLA_QK_DIM ** -0.5
ROPE_THETA = 10000.0
SB_HEADS = 8
SB_DIM = 64
SB_WIDTH = SB_HEADS * SB_DIM
SB_SCALE = SB_DIM ** -0.5
N_EXPERTS = 32
TOP_K = 4
SWIGLU_LIMIT = 7.0
SWIGLU_ALPHA = 1.702
EPS = 1e-6
NEG = -1e30
SB_DEAD = -105.0

LANES = 128
HEAD_PAD = 128
VMEM_LIMIT = 56 * 1024 * 1024

ROW_TILE = 256
ATT_BLK = 256
MLA_GROUP = 4
PAGES_PER_STEP = 16
MOE_ROWS = 256

C_QA = 0
C_CKV = C_QA + Q_RANK
C_KR = C_CKV + KV_RANK
C_KRSW = C_KR + LANES
C_QS = C_KRSW + LANES
C_KS = C_QS + SB_WIDTH
C_VS = C_KS + SB_WIDTH
C_GATE = C_VS + SB_WIDTH


def _nt(a, b):
    return lax.dot_general(a, b, (((1,), (1,)), ((), ())), preferred_element_type=F32)


def _mm(a, b):
    return jnp.dot(a, b, preferred_element_type=F32)


def _split_bf16(x):
    hi = x.astype(BF16)
    lo = (x - hi.astype(F32)).astype(BF16)
    return hi, lo


def _rms(x):
    return lax.rsqrt(jnp.mean(x * x, axis=-1, keepdims=True) + EPS)


def _cparams(n_axes):
    return pltpu.CompilerParams(dimension_semantics=("arbitrary",) * n_axes,
                                vmem_limit_bytes=VMEM_LIMIT)


def _ada_kernel(c_ref, w_ref, b_ref, o_ref):
    c = c_ref[...]
    s = c * jax.nn.sigmoid(c)
    o_ref[...] = _mm(s.astype(BF16), w_ref[...]) + b_ref[...]


def _ada_call(c, w, b):
    n, d = c.shape
    cols = w.shape[1]
    return pl.pallas_call(
        _ada_kernel,
        grid=(cols // d,),
        in_specs=[pl.BlockSpec((n, d), lambda j: (0, 0)),
                  pl.BlockSpec((d, d), lambda j: (0, j)),
                  pl.BlockSpec((1, d), lambda j: (0, j))],
        out_specs=pl.BlockSpec((n, d), lambda j: (0, j)),
        out_shape=jax.ShapeDtypeStruct((n, cols), F32),
        compiler_params=_cparams(1),
        name="adaln",
    )(c, w, b)


def _proj_kernel(x_ref, sh_ref, sc_ref, cos_ref, sin_ref, gattn_ref, win_ref, gqa_ref, wqb_ref,
                 gq_ref, gqsw_ref, pq_ref, gkva_ref, wkb_ref, gk_ref, pk_ref, gkr_ref, gkrsw_ref,
                 ckv_o, krope_o, sbk_o, sbv_o, qm_o, km_o, ckvb_o, qs_o, ks_o, vs_o, gates_o):
    x = x_ref[...]
    a = x * _rms(x) * gattn_ref[...] * (1.0 + sc_ref[0]) + sh_ref[0]
    h = _mm(a.astype(BF16), win_ref[...])
    cosb = cos_ref[...]
    sinb = sin_ref[...]

    qa = h[:, C_QA:C_QA + Q_RANK]
    qa_n = qa * _rms(qa) * gqa_ref[...]
    qq = _mm(qa_n.astype(BF16), wqb_ref[...])
    gq = gq_ref[...]
    gqsw = gqsw_ref[...]
    pq = pq_ref[...]
    width = MLA_HEADS * HEAD_PAD
    for hd in range(MLA_HEADS):
        lo_c = hd * HEAD_PAD
        q = qq[:, lo_c:lo_c + HEAD_PAD]
        qsw = qq[:, width + lo_c:width + lo_c + HEAD_PAD]
        hi, lo = _split_bf16(q * q)
        rin = lax.rsqrt(_mm(hi, pq) + _mm(lo, pq) + EPS)
        qm_o[:, lo_c:lo_c + HEAD_PAD] = ((q * gq * cosb + qsw * gqsw * sinb) * rin).astype(qm_o.dtype)

    ckv_raw = h[:, C_CKV:C_CKV + KV_RANK]
    ckv = ckv_raw * _rms(ckv_raw) * gkva_ref[...]
    ckv_o[...] = ckv
    ckvb = ckv.astype(BF16)
    ckvb_o[...] = ckvb
    kr = h[:, C_KR:C_KR + LANES]
    krsw = h[:, C_KRSW:C_KRSW + LANES]
    kr_rin = lax.rsqrt(jnp.sum(kr * kr, axis=-1, keepdims=True) * (1.0 / ROPE_DIM) + EPS)
    krope = (kr * gkr_ref[...] * cosb + krsw * gkrsw_ref[...] * sinb) * kr_rin
    krope_o[...] = krope[:, NOPE_DIM:NOPE_DIM + ROPE_DIM]

    kexp = _mm(ckvb, wkb_ref[...])
    gk = gk_ref[...]
    pk = pk_ref[...]
    for hd in range(MLA_HEADS):
        lo_c = hd * HEAD_PAD
        k = kexp[:, lo_c:lo_c + HEAD_PAD]
        hi, lo = _split_bf16(k * k)
        rin = lax.rsqrt(_mm(hi, pk) + _mm(lo, pk) + EPS)
        km_o[:, lo_c:lo_c + HEAD_PAD] = (k * gk * rin + krope).astype(km_o.dtype)

    qs_o[...] = (h[:, C_QS:C_QS + SB_WIDTH] * SB_SCALE).astype(qs_o.dtype)
    ks = h[:, C_KS:C_KS + SB_WIDTH]
    vs = h[:, C_VS:C_VS + SB_WIDTH]
    sbk_o[...] = ks
    sbv_o[...] = vs
    ks_o[...] = ks.astype(BF16)
    vs_o[...] = vs.astype(BF16)
    gates_o[...] = jax.nn.sigmoid(h[:, C_GATE:]).astype(BF16)


def _row_specs(rows, tm, per_seq_rows):
    if per_seq_rows is not None:
        tps = per_seq_rows // tm
        mod = lambda d: pl.BlockSpec((1, 1, d), lambda i: (i // tps, 0, 0))
        tab = pl.BlockSpec((tm, LANES), lambda i: (i % tps, 0))
    else:
        mod = lambda d: pl.BlockSpec((1, tm, d), lambda i: (i, 0, 0))
        tab = pl.BlockSpec((tm, LANES), lambda i: (i, 0))
    return mod, tab


def _proj_call(x, sh, sc, cosb, sinb, w, per_seq_rows, q_dtype):
    rows, d = x.shape
    tm = min(ROW_TILE, rows)
    mod, tab = _row_specs(rows, tm, per_seq_rows)
    full = lambda arr: pl.BlockSpec(arr.shape, lambda i: (0,) * arr.ndim)
    rowblk = lambda c: pl.BlockSpec((tm, c), lambda i: (i, 0))
    hw = MLA_HEADS * HEAD_PAD
    consts = [w["g_attn"], w["w_in"], w["g_qa"], w["w_qb"], w["gq"], w["gqsw"], w["pq"], w["g_kva"],
              w["w_kb"], w["gk"], w["pk"], w["gkr"], w["gkrsw"]]
    out_cols = [(KV_RANK, F32), (ROPE_DIM, F32), (SB_WIDTH, F32), (SB_WIDTH, F32), (hw, q_dtype),
                (hw, q_dtype), (KV_RANK, BF16), (SB_WIDTH, q_dtype), (SB_WIDTH, BF16), (SB_WIDTH, BF16),
                (2 * d, BF16)]
    return pl.pallas_call(
        _proj_kernel,
        grid=(rows // tm,),
        in_specs=[rowblk(d), mod(d), mod(d), tab, tab] + [full(c) for c in consts],
        out_specs=[rowblk(c) for c, _ in out_cols],
        out_shape=[jax.ShapeDtypeStruct((rows, c), dt) for c, dt in out_cols],
        compiler_params=_cparams(1),
        name="in_proj",
    )(x, sh, sc, cosb, sinb, *consts)


def _log1m(z):
    return jnp.minimum(-z, 0.0) - jnp.log(1.0 + jnp.exp(-jnp.abs(z)))


def _suffix_sums(log1m, tri):
    hi, lo = _split_bf16(log1m)
    return _mm(hi, tri) + _mm(lo, tri)


def _prompt_attn_kernel(qm_ref, km_ref, ckv_ref, qs_ref, ks_ref, vs_ref, wv_ref, tri_ref,
                        ym_ref, ys_ref, m_sc, l_sc, om_sc, ls_sc, os_sc, *, blk, n_blk):
    qi = pl.program_id(2)
    lane = lax.broadcasted_iota(jnp.int32, (blk, LANES), 1)
    tri = tri_ref[...]
    qs_pair = qs_ref[0]
    zero = jnp.zeros_like(qs_pair)
    half = LANES // 2
    heads = range(2)
    q_s = [jnp.where(lane < half, qs_pair, zero), jnp.where(lane >= half, qs_pair, zero)]
    q_m = [qm_ref[0, :, 0:HEAD_PAD], qm_ref[0, :, HEAD_PAD:2 * HEAD_PAD]]
    ones_v = jnp.ones((blk, LANES), BF16)

    def key_start(j):
        return pl.multiple_of(jnp.clip(j, 0, n_blk - 1) * blk, blk)

    def rel_pos(j):
        row = lax.broadcasted_iota(jnp.int32, (blk, blk), 0)
        col = lax.broadcasted_iota(jnp.int32, (blk, blk), 1)
        return col - row + (j - qi) * blk

    def mla_group(js, masked):
        k0s = [key_start(j) for j in js]
        vals = [jnp.concatenate([ckv_ref[0, pl.ds(k0, blk), :], ones_v], axis=1) for k0 in k0s]
        ss_h = [[_nt(q_m[hh], km_ref[0, pl.ds(k0, blk), hh * HEAD_PAD:(hh + 1) * HEAD_PAD]) for k0 in k0s]
                for hh in heads]
        if masked:
            causal = [rel_pos(j) <= 0 for j in js]
            ss_h = [[jnp.where(c, s, NEG) for c, s in zip(causal, ss)] for ss in ss_h]
        m_prevs = [m_sc[hh] for hh in heads]
        m_news = [jnp.maximum(m_prevs[hh], jnp.max(jnp.concatenate(ss_h[hh], axis=1), axis=-1, keepdims=True))
                  for hh in heads]
        for hh in heads:
            m_rep = jnp.concatenate([m_news[hh]] * (blk // LANES), axis=1)
            alpha = jnp.exp(m_prevs[hh] - m_news[hh])
            acc = jnp.concatenate([alpha * om_sc[hh], alpha * l_sc[hh]], axis=1)
            for idx, (s, val) in enumerate(zip(ss_h[hh], vals)):
                p = jnp.exp(s - m_rep)
                if masked:
                    p = jnp.where(causal[idx], p, 0.0)
                acc = acc + _mm(p.astype(BF16), val)
            om_sc[hh] = acc[:, 0:KV_RANK]
            l_sc[hh] = acc[:, KV_RANK:]
            m_sc[hh] = m_news[hh]

    def sb_group(js, masked):
        k0s = [key_start(j) for j in js]
        kss = [ks_ref[0, pl.ds(k0, blk), :] for k0 in k0s]
        vss = [vs_ref[0, pl.ds(k0, blk), :] for k0 in k0s]
        zs = [[_nt(q_s[hh], ks) for ks in kss] for hh in heads]
        l1 = [[_log1m(z) for z in zs[hh]] for hh in heads]
        if masked:
            strict = [jnp.logical_and(rel_pos(j) < 0, j >= 0) for j in js]
            l1 = [[jnp.where(st, v, 0.0) for st, v in zip(strict, l1[hh])] for hh in heads]
        cs = [[_suffix_sums(v, tri) for v in l1[hh]] for hh in heads]
        for hh in heads:
            ls = ls_sc[hh]
            acc = os_sc[hh]
            for idx in range(len(js)):
                a = jnp.exp(zs[hh][idx] + cs[hh][idx] + ls)
                if masked:
                    a = jnp.where(strict[idx], a, 0.0)
                acc = acc + _mm(a.astype(BF16), vss[idx])
                ls = ls + cs[hh][idx][:, 0:1]
            os_sc[hh] = acc
            ls_sc[hh] = ls

    for hh in heads:
        m_sc[hh] = jnp.full((blk, LANES), NEG, F32)
        l_sc[hh] = jnp.zeros((blk, LANES), F32)
        om_sc[hh] = jnp.zeros((blk, KV_RANK), F32)
        ls_sc[hh] = jnp.zeros((blk, 1), F32)
        os_sc[hh] = jnp.zeros((blk, LANES), F32)

    n_main = qi // MLA_GROUP
    mla_group([n_main * MLA_GROUP + u for u in range(MLA_GROUP)], True)

    def mla_body(it, carry):
        mla_group([it * MLA_GROUP + u for u in range(MLA_GROUP)], False)
        return carry

    lax.fori_loop(0, n_main, mla_body, 0)

    sb_group([qi, qi - 1], True)

    def sb_alive():
        return jnp.max(jnp.maximum(ls_sc[0], ls_sc[1])) > SB_DEAD

    def sb_cond(carry):
        j, alive = carry
        return jnp.logical_and(j >= 0, alive)

    def sb_body(carry):
        j, _ = carry
        sb_group([j], False)
        return j - 1, sb_alive()

    lax.while_loop(sb_cond, sb_body, (qi - 2, sb_alive()))

    ylat = [(om_sc[hh] / l_sc[hh]).astype(BF16) for hh in heads]
    ym_ref[0] = _mm(jnp.concatenate(ylat, axis=1), wv_ref[0]).astype(BF16)
    ys_ref[0] = jnp.where(lane < half, os_sc[0], os_sc[1]).astype(BF16)


def _prompt_attn_call(qm, km, ckvb, qs, ks, vs, wv_pair, tri, blk):
    b, s, _ = qm.shape
    hp = MLA_HEADS // 2
    qspec = lambda c: pl.BlockSpec((1, blk, c), lambda bi, h, qi: (bi, qi, h))
    kspec = lambda c: pl.BlockSpec((1, s, c), lambda bi, h, qi: (bi, 0, h))
    return pl.pallas_call(
        functools.partial(_prompt_attn_kernel, blk=blk, n_blk=s // blk),
        grid=(b, hp, s // blk),
        in_specs=[qspec(2 * HEAD_PAD), kspec(2 * HEAD_PAD),
                  pl.BlockSpec((1, s, KV_RANK), lambda bi, h, qi: (bi, 0, 0)),
                  qspec(LANES), kspec(LANES), kspec(LANES),
                  pl.BlockSpec((1, 2 * KV_RANK, LANES), lambda bi, h, qi: (h, 0, 0)),
                  pl.BlockSpec((blk, blk), lambda bi, h, qi: (0, 0))],
        out_specs=[qspec(LANES), qspec(LANES)],
        out_shape=[jax.ShapeDtypeStruct((b, s, MLA_HEADS * V_DIM), BF16),
                   jax.ShapeDtypeStruct((b, s, SB_WIDTH), BF16)],
        scratch_shapes=[pltpu.VMEM((2, blk, LANES), F32), pltpu.VMEM((2, blk, LANES), F32),
                        pltpu.VMEM((2, blk, KV_RANK), F32), pltpu.VMEM((2, blk, 1), F32),
                        pltpu.VMEM((2, blk, LANES), F32)],
        compiler_params=_cparams(3),
        name="prompt_attn",
    )(qm, km, ckvb, qs, ks, vs, wv_pair, tri)


def _sample_attn_kernel(pt_ref, qm_ref, kmn_ref, ckvn_ref, qs_ref, ksn_ref, vsn_ref,
                        wabs_ref, wkb_ref, wv_ref, emean_ref, tri_ref, *rest,
                        t_new, page, n_grp, n_pages, nseq, layer):
    n_pg = PAGES_PER_STEP
    ckv_refs = rest[0:n_pg]
    kr_refs = rest[n_pg:2 * n_pg]
    k_hbm, v_hbm = rest[2 * n_pg:2 * n_pg + 2]
    ym_ref, ys_ref = rest[2 * n_pg + 2:2 * n_pg + 4]
    (qabs_sc, qr_sc, qbd_sc, m_sc, l_sc, om_sc, ls_sc, os_sc, kbuf, vbuf, sem) = rest[2 * n_pg + 4:]

    b = pl.program_id(0)
    p = pl.program_id(1)
    slot = b % 2
    rows = MLA_HEADS * t_new
    hw = MLA_HEADS * HEAD_PAD
    tri = tri_ref[...]

    def head_mask(cols, per_head):
        r = lax.broadcasted_iota(jnp.int32, (rows, cols), 0) // t_new
        c = lax.broadcasted_iota(jnp.int32, (rows, cols), 1) // per_head
        return r == c

    def sb_copies(seq, grp, sl):
        out = []
        for j in range(n_pg):
            phys = pt_ref[seq * n_pages + (n_pages - 1 - (grp * n_pg + j))]
            out.append(pltpu.make_async_copy(k_hbm.at[layer, phys], kbuf.at[sl, j], sem.at[0, sl]))
            out.append(pltpu.make_async_copy(v_hbm.at[layer, phys], vbuf.at[sl, j], sem.at[1, sl]))
        return out

    def sb_group(sl):
        for cidx in range(n_pg // 2):
            newer, older = 2 * cidx, 2 * cidx + 1

            @pl.when(jnp.max(ls_sc[...]) > SB_DEAD)
            def _chunk():
                flat = lambda buf, j: buf[sl, j].reshape(SB_WIDTH, page).astype(BF16)
                kt = jnp.concatenate([flat(kbuf, older), flat(kbuf, newer)], axis=1)
                vt = jnp.concatenate([flat(vbuf, older), flat(vbuf, newer)], axis=1)
                z = _mm(qbd_sc[...], kt)
                c = _suffix_sums(_log1m(z), tri)
                ls = ls_sc[...]
                a = jnp.exp(z + c + ls).astype(BF16)
                os_sc[...] = os_sc[...] + _nt(a, vt)
                ls_sc[...] = ls + c[:, 0:1]

    @pl.when(p == 0)
    def _init():
        @pl.when(b == 0)
        def _():
            for cp in sb_copies(0, 0, 0):
                cp.start()

        @pl.when(b + 1 < nseq)
        def _():
            for cp in sb_copies(b + 1, 0, 1 - slot):
                cp.start()

        pad = page - t_new
        padk = lambda ref: jnp.concatenate(
            [ref[...], jnp.zeros((pad, ref.shape[1]), F32)], axis=0).astype(BF16)
        qm = jnp.concatenate([qm_ref[...]] * MLA_HEADS, axis=0)
        qbd_m = jnp.where(head_mask(hw, HEAD_PAD), qm, 0.0).astype(BF16)
        qabs_sc[...] = _mm(qbd_m, wabs_ref[...]).astype(BF16)
        qr = [qm_ref[:, hd * HEAD_PAD + NOPE_DIM:hd * HEAD_PAD + NOPE_DIM + ROPE_DIM]
              for hd in range(MLA_HEADS)]
        qr_sc[...] = jnp.concatenate(qr, axis=0).astype(BF16)
        qs = jnp.concatenate([qs_ref[...]] * SB_HEADS, axis=0)
        qbd_s = jnp.where(head_mask(SB_WIDTH, SB_DIM), qs, 0.0).astype(BF16)
        qbd_sc[...] = qbd_s

        qrow = lax.broadcasted_iota(jnp.int32, (rows, page), 0) % t_new
        kcol = lax.broadcasted_iota(jnp.int32, (rows, page), 1)
        causal = kcol <= qrow
        strict = kcol < qrow
        s = jnp.where(causal, _nt(qbd_m, padk(kmn_ref)), NEG)
        m_new = jnp.broadcast_to(jnp.max(s, axis=-1, keepdims=True), (rows, LANES))
        pr = jnp.where(causal, jnp.exp(s - m_new), 0.0)
        pv = _mm(pr.astype(BF16), jnp.concatenate([padk(ckvn_ref), jnp.ones((page, LANES), BF16)], axis=1))
        m_sc[...] = m_new
        om_sc[...] = pv[:, 0:KV_RANK]
        l_sc[...] = pv[:, KV_RANK:]
        z = _nt(qbd_s, padk(ksn_ref))
        log1m = jnp.where(strict, _log1m(z), 0.0)
        c = _suffix_sums(log1m, tri[0:page, 0:page])
        a = jnp.where(strict, jnp.exp(z + c), 0.0)
        os_sc[...] = _mm(a.astype(BF16), padk(vsn_ref))
        ls_sc[...] = c[:, 0:1]

        for cp in sb_copies(b, 0, slot):
            cp.wait()
        sb_group(slot)

    @pl.when(jnp.logical_and(p > 0, jnp.max(ls_sc[...]) > SB_DEAD))
    def _older():
        for cp in sb_copies(b, p, slot):
            cp.start()
        for cp in sb_copies(b, p, slot):
            cp.wait()
        sb_group(slot)

    qabs = qabs_sc[...]
    qr = qr_sc[...]
    wkb = wkb_ref[...]
    emean = emean_ref[...]
    ones_v = jnp.ones((2 * page, LANES), BF16)
    chunks = [(2 * c + 1, 2 * c) for c in range(n_pg // 2)]
    ckvs = [jnp.concatenate([ckv_refs[o][0, 0], ckv_refs[n][0, 0]], axis=0).astype(BF16) for o, n in chunks]
    krts = [jnp.concatenate([kr_refs[o][0, 0], kr_refs[n][0, 0]], axis=1).astype(BF16) for o, n in chunks]
    kexps = [_mm(ckv, wkb) for ckv in ckvs]
    sss = [_nt(emean, (k * k).astype(BF16)) for k in kexps]
    ss = [_nt(qabs, ckv) * lax.rsqrt(msq + EPS) + _mm(qr, krt) for ckv, msq, krt in zip(ckvs, sss, krts)]
    m_prev = m_sc[...]
    m_new = jnp.maximum(m_prev, jnp.max(jnp.concatenate(ss, axis=1), axis=-1, keepdims=True))
    m_rep = jnp.concatenate([m_new] * (2 * page // LANES), axis=1)
    alpha = jnp.exp(m_prev - m_new)
    acc = jnp.concatenate([alpha * om_sc[...], alpha * l_sc[...]], axis=1)
    for s_c, ckv in zip(ss, ckvs):
        pr = jnp.exp(s_c - m_rep)
        acc = acc + _mm(pr.astype(BF16), jnp.concatenate([ckv, ones_v], axis=1))
    om_sc[...] = acc[:, 0:KV_RANK]
    l_sc[...] = acc[:, KV_RANK:]
    m_sc[...] = m_new

    @pl.when(p == n_grp - 1)
    def _fin():
        ylat = (om_sc[...] / l_sc[...]).astype(BF16)
        full_m = jnp.where(head_mask(MLA_HEADS * V_DIM, V_DIM), _mm(ylat, wv_ref[...]), 0.0)
        full_s = jnp.where(head_mask(SB_WIDTH, SB_DIM), os_sc[...], 0.0)
        ym = full_m[0:t_new]
        ys = full_s[0:t_new]
        for hd in range(1, MLA_HEADS):
            ym = ym + full_m[hd * t_new:(hd + 1) * t_new]
            ys = ys + full_s[hd * t_new:(hd + 1) * t_new]
        ym_ref[...] = ym.astype(BF16)
        ys_ref[...] = ys.astype(BF16)


def _sample_attn_call(page_table, qm, kmn, ckvn, qs, ksn, vsn, w, tri, caches, layer, t_new):
    cache_ckv, cache_krt, cache_kt, cache_vt = caches
    nseq, n_pages = page_table.shape
    page = cache_ckv.shape[2]
    n_pg = PAGES_PER_STEP
    n_grp = n_pages // n_pg
    rows = MLA_HEADS * t_new
    pt = page_table.reshape(-1)

    seqblk = lambda c: pl.BlockSpec((t_new, c), lambda b, p, pt_r: (b, 0))
    full = lambda arr: pl.BlockSpec(arr.shape, lambda b, p, pt_r: (0,) * arr.ndim)

    def page_spec(j, shape):
        def imap(b, p, pt_r):
            return (layer, pt_r[b * n_pages + (n_pages - 1 - (p * n_pg + j))], 0, 0)
        return pl.BlockSpec((1, 1) + shape, imap)

    consts = [w["wabs"], w["wkb_s"], w["wv_s"], w["emean"], tri]
    page_specs = ([page_spec(j, (page, KV_RANK)) for j in range(n_pg)]
                  + [page_spec(j, (ROPE_DIM, page)) for j in range(n_pg)])
    page_args = [cache_ckv] * n_pg + [cache_krt] * n_pg
    hbm = pl.BlockSpec(memory_space=pl.ANY)
    hw = MLA_HEADS * HEAD_PAD
    sb_buf = pltpu.VMEM((2, n_pg, SB_HEADS, SB_DIM, page), F32)
    grid_spec = pltpu.PrefetchScalarGridSpec(
        num_scalar_prefetch=1,
        grid=(nseq, n_grp),
        in_specs=[seqblk(hw), seqblk(hw), seqblk(KV_RANK), seqblk(SB_WIDTH), seqblk(SB_WIDTH),
                  seqblk(SB_WIDTH)] + [full(c) for c in consts] + page_specs + [hbm, hbm],
        out_specs=[seqblk(MLA_HEADS * V_DIM), seqblk(SB_WIDTH)],
        scratch_shapes=[pltpu.VMEM((rows, KV_RANK), BF16), pltpu.VMEM((rows, ROPE_DIM), BF16),
                        pltpu.VMEM((rows, SB_WIDTH), BF16),
                        pltpu.VMEM((rows, LANES), F32), pltpu.VMEM((rows, LANES), F32),
                        pltpu.VMEM((rows, KV_RANK), F32), pltpu.VMEM((rows, 1), F32),
                        pltpu.VMEM((rows, SB_WIDTH), F32), sb_buf, sb_buf,
                        pltpu.SemaphoreType.DMA((2, 2))],
    )
    return pl.pallas_call(
        functools.partial(_sample_attn_kernel, t_new=t_new, page=page, n_grp=n_grp, n_pages=n_pages,
                          nseq=nseq, layer=layer),
        grid_spec=grid_spec,
        out_shape=[jax.ShapeDtypeStruct((nseq * t_new, MLA_HEADS * V_DIM), BF16),
                   jax.ShapeDtypeStruct((nseq * t_new, SB_WIDTH), BF16)],
        compiler_params=_cparams(2),
        name="sample_attn",
    )(pt, qm, kmn, ckvn, qs, ksn, vsn, *consts, *page_args, cache_kt, cache_vt)


def _out_kernel(x_ref, ym_ref, ys_ref, gates_ref, gt1_ref, sh2_ref, sc2_ref, woa_ref, wob_ref, wout_ref,
                gffn_ref, wr_ref, br_ref, x1_o, f_o, topi_o, prob_o):
    d = x_ref.shape[1]
    ga = gates_ref[:, 0:d].astype(F32)
    gb = gates_ref[:, d:2 * d].astype(F32)
    mix = ga * _mm(ym_ref[...], woa_ref[...]) + gb * _mm(ys_ref[...], wob_ref[...])
    x1 = x_ref[...] + gt1_ref[0] * _mm(mix.astype(BF16), wout_ref[...])
    x1_o[...] = x1
    f = x1 * _rms(x1) * gffn_ref[...] * (1.0 + sc2_ref[0]) + sh2_ref[0]
    f_o[...] = f.astype(BF16)

    logits = jnp.dot(f, wr_ref[...], preferred_element_type=F32,
                     precision=lax.Precision.HIGHEST) + br_ref[...]
    lane = lax.broadcasted_iota(jnp.int32, logits.shape, 1)
    lanef = lane.astype(F32)
    neg_inf = jnp.float32(-jnp.inf)
    work = jnp.where(lane < N_EXPERTS, logits, neg_inf)
    topv = []
    topi = jnp.zeros(logits.shape, F32)
    for k in range(TOP_K):
        mx = jnp.max(work, axis=-1, keepdims=True)
        idx = jnp.min(jnp.where(work == mx, lanef, float(LANES)), axis=-1, keepdims=True)
        topv.append(mx)
        topi = jnp.where(lane == k, idx, topi)
        work = jnp.where(lanef == idx, neg_inf, work)
    ex = [jnp.exp(v - topv[0]) for v in topv]
    den = ex[0]
    for e in ex[1:]:
        den = den + e
    prob = jnp.zeros(logits.shape, F32)
    for k in range(TOP_K):
        prob = jnp.where(lane == k, ex[k] / den, prob)
    topi_o[...] = topi.astype(jnp.int32)
    prob_o[...] = prob


def _out_call(x, ym, ys, gates, gt1, sh2, sc2, w, per_seq_rows):
    rows, d = x.shape
    tm = min(ROW_TILE, rows)
    mod, _ = _row_specs(rows, tm, per_seq_rows)
    full = lambda arr: pl.BlockSpec(arr.shape, lambda i: (0,) * arr.ndim)
    rowblk = lambda c: pl.BlockSpec((tm, c), lambda i: (i, 0))
    consts = [w["w_o_mla"], w["w_o_sb"], w["w_out"], w["g_ffn"], w["w_router"], w["b_router"]]
    return pl.pallas_call(
        _out_kernel,
        grid=(rows // tm,),
        in_specs=[rowblk(d), rowblk(ym.shape[1]), rowblk(ys.shape[1]), rowblk(2 * d),
                  mod(d), mod(d), mod(d)] + [full(c) for c in consts],
        out_specs=[rowblk(d), rowblk(d), rowblk(LANES), rowblk(LANES)],
        out_shape=[jax.ShapeDtypeStruct((rows, d), F32), jax.ShapeDtypeStruct((rows, d), BF16),
                   jax.ShapeDtypeStruct((rows, LANES), jnp.int32),
                   jax.ShapeDtypeStruct((rows, LANES), F32)],
        compiler_params=_cparams(1),
        name="out_proj",
    )(x, ym, ys, gates, gt1, sh2, sc2, *consts)


def _moe_kernel(be_ref, nu_ref, x_ref, wgu_ref, bgu_ref, wd_ref, bd_ref, o_ref, wgu_bf, wd_bf):
    i = pl.program_id(0)
    d_ff = wd_ref.shape[1]

    @pl.when(jnp.logical_or(i == 0, be_ref[i] != be_ref[jnp.maximum(i - 1, 0)]))
    def _cast():
        wgu_bf[...] = wgu_ref[0].astype(BF16)
        wd_bf[...] = wd_ref[0].astype(BF16)

    @pl.when(i < nu_ref[0])
    def _compute():
        gu = _mm(x_ref[...], wgu_bf[...]) + bgu_ref[0]
        gate = jnp.minimum(gu[:, :d_ff], SWIGLU_LIMIT)
        up = jnp.clip(gu[:, d_ff:], -SWIGLU_LIMIT, SWIGLU_LIMIT)
        act = (up + 1.0) * gate * jax.nn.sigmoid(SWIGLU_ALPHA * gate)
        o_ref[...] = (_mm(act.astype(BF16), wd_bf[...]) + bd_ref[0]).astype(o_ref.dtype)

    @pl.when(i >= nu_ref[0])
    def _unused():
        o_ref[...] = jnp.zeros(o_ref.shape, o_ref.dtype)


def _moe_call(blk_e, n_used, xb, w_gu, b_gu, w_down, b_down):
    n_rows, d = xb.shape
    mb = MOE_ROWS
    d_ff = w_down.shape[1]
    grid_spec = pltpu.PrefetchScalarGridSpec(
        num_scalar_prefetch=2,
        grid=(n_rows // mb,),
        in_specs=[pl.BlockSpec((mb, d), lambda i, be, nu: (i, 0)),
                  pl.BlockSpec((1, d, 2 * d_ff), lambda i, be, nu: (be[i], 0, 0)),
                  pl.BlockSpec((1, 1, 2 * d_ff), lambda i, be, nu: (be[i], 0, 0)),
                  pl.BlockSpec((1, d_ff, d), lambda i, be, nu: (be[i], 0, 0)),
                  pl.BlockSpec((1, 1, d), lambda i, be, nu: (be[i], 0, 0))],
        out_specs=pl.BlockSpec((mb, d), lambda i, be, nu: (i, 0)),
        scratch_shapes=[pltpu.VMEM((d, 2 * d_ff), BF16), pltpu.VMEM((d_ff, d), BF16)],
    )
    return pl.pallas_call(
        _moe_kernel,
        grid_spec=grid_spec,
        out_shape=jax.ShapeDtypeStruct((n_rows, d), BF16),
        compiler_params=_cparams(1),
        name="moe_mlp",
    )(blk_e, n_used, xb, w_gu, b_gu, w_down, b_down)


def _route(top_i, n, mb):
    nk = n * TOP_K
    flat_e = top_i.reshape(-1)
    order = jnp.argsort(flat_e).astype(jnp.int32)
    rank = jnp.argsort(order).astype(jnp.int32)
    experts = jnp.arange(N_EXPERTS, dtype=jnp.int32)
    counts = jnp.sum((flat_e[:, None] == experts[None, :]).astype(jnp.int32), axis=0)
    padded = (counts + mb - 1) // mb * mb
    start = jnp.cumsum(counts) - counts
    pend = jnp.cumsum(padded)
    pstart = pend - padded
    dest = pstart[flat_e] + rank - start[flat_e]
    n_blocks = -(-nk // mb) + N_EXPERTS
    blk_e = jnp.minimum(jnp.sum((pend[None, :] <= (jnp.arange(n_blocks, dtype=jnp.int32) * mb)[:, None])
                                .astype(jnp.int32), axis=1), N_EXPERTS - 1)
    row = jnp.arange(n_blocks * mb, dtype=jnp.int32)
    row_e = jnp.repeat(blk_e, mb)
    within = row - pstart[row_e]
    src = jnp.clip(start[row_e] + within, 0, nk - 1)
    tok = jnp.where(within < counts[row_e], order[src] // TOP_K, n).astype(jnp.int32)
    n_used = (pend[-1] // mb).astype(jnp.int32).reshape(1)
    return tok, dest.astype(jnp.int32), blk_e.astype(jnp.int32), n_used


def _combine_kernel(x1_ref, gt2_ref, prob_ref, *rest):
    yk_refs, o_ref = rest[:TOP_K], rest[TOP_K]
    prob = prob_ref[...]
    ff = prob[:, 0:1] * yk_refs[0][...].astype(F32)
    for k in range(1, TOP_K):
        ff = ff + prob[:, k:k + 1] * yk_refs[k][...].astype(F32)
    o_ref[...] = x1_ref[...] + gt2_ref[0] * ff


def _combine_call(x1, gt2, prob, yks, per_seq_rows):
    rows, d = x1.shape
    tm = min(ROW_TILE, rows)
    mod, _ = _row_specs(rows, tm, per_seq_rows)
    rowblk = lambda c: pl.BlockSpec((tm, c), lambda i: (i, 0))
    return pl.pallas_call(
        _combine_kernel,
        grid=(rows // tm,),
        in_specs=[rowblk(d), mod(d), rowblk(LANES)] + [rowblk(d)] * TOP_K,
        out_specs=rowblk(d),
        out_shape=jax.ShapeDtypeStruct((rows, d), F32),
        compiler_params=_cparams(1),
        name="combine",
    )(x1, gt2, prob, *yks)


def _prep_weights(l, w_ada, b_ada, g_attn, w_in, g_qa, w_q_b, g_kva, w_kv_b, g_qn, g_qr, g_kn, g_kr,
                  w_o_mla, w_o_sb, w_out, g_ffn, w_router, b_router):
    d = w_in.shape[1]
    half = ROPE_DIM // 2
    z = lambda *s: jnp.zeros(s, F32)
    wi = w_in[l]
    o = 0
    cuts = []
    for width in (Q_RANK, KV_RANK, ROPE_DIM, SB_WIDTH, SB_WIDTH, SB_WIDTH, d, d):
        cuts.append(wi[:, o:o + width])
        o += width
    c_qa, c_ckv, c_kr, c_qs, c_ks, c_vs, c_ga, c_gb = cuts
    tail = LANES - NOPE_DIM - ROPE_DIM
    kr_blk = jnp.concatenate([z(d, NOPE_DIM), c_kr, z(d, tail)], axis=1)
    krsw_blk = jnp.concatenate([z(d, NOPE_DIM), c_kr[:, half:], c_kr[:, :half], z(d, tail)], axis=1)
    w_in2 = jnp.concatenate([c_qa, c_ckv, kr_blk, krsw_blk, c_qs, c_ks, c_vs, c_ga, c_gb], axis=1)

    wq = w_q_b[l].reshape(Q_RANK, MLA_HEADS, MLA_QK_DIM)
    wq_main = jnp.concatenate([wq, z(Q_RANK, MLA_HEADS, tail)], axis=-1)
    wq_sw = jnp.concatenate([z(Q_RANK, MLA_HEADS, NOPE_DIM), wq[:, :, NOPE_DIM + half:],
                             wq[:, :, NOPE_DIM:NOPE_DIM + half], z(Q_RANK, MLA_HEADS, tail)], axis=-1)
    w_qb = jnp.concatenate([wq_main.reshape(Q_RANK, -1), wq_sw.reshape(Q_RANK, -1)], axis=1)

    wkv = w_kv_b[l].reshape(KV_RANK, MLA_HEADS, NOPE_DIM + V_DIM)
    wk = wkv[:, :, :NOPE_DIM]
    wv = wkv[:, :, NOPE_DIM:]
    w_kb = jnp.concatenate([wk, z(KV_RANK, MLA_HEADS, HEAD_PAD - NOPE_DIM)], axis=-1).reshape(KV_RANK, -1)
    wv_t = wv.transpose(1, 0, 2)
    zb = z(MLA_HEADS // 2, KV_RANK, V_DIM)
    wv_pair = jnp.concatenate([jnp.concatenate([wv_t[0::2], zb], axis=2),
                               jnp.concatenate([zb, wv_t[1::2]], axis=2)], axis=1)
    wabs = jnp.concatenate([(wk * g_kn[l][None, None, :]).transpose(1, 2, 0),
                            z(MLA_HEADS, HEAD_PAD - NOPE_DIM, KV_RANK)], axis=1).reshape(-1, KV_RANK)

    lane = np.arange(LANES)
    grp_n = lane < NOPE_DIM
    grp_r = (lane >= NOPE_DIM) & (lane < MLA_QK_DIM)
    pq = (np.outer(grp_n, grp_n) / NOPE_DIM + np.outer(grp_r, grp_r) / ROPE_DIM).astype(np.float32)
    pk = (np.outer(grp_n, np.ones(LANES, bool)) / NOPE_DIM).astype(np.float32)
    tailz = z(tail)
    row1 = lambda v: v.reshape(1, -1)
    gqr, gkr_ = g_qr[l], g_kr[l]
    return {
        "w_ada": w_ada[l].astype(BF16), "b_ada": row1(b_ada[l]),
        "g_attn": row1(g_attn[l]), "w_in": w_in2.astype(BF16), "g_qa": row1(g_qa[l]),
        "w_qb": w_qb.astype(BF16),
        "gq": row1(jnp.concatenate([g_qn[l], gqr, tailz]) * MLA_SCALE),
        "gqsw": row1(jnp.concatenate([z(NOPE_DIM), gqr[half:], gqr[:half], tailz]) * MLA_SCALE),
        "pq": jnp.asarray(pq, BF16), "g_kva": row1(g_kva[l]), "w_kb": w_kb.astype(BF16),
        "gk": row1(jnp.concatenate([g_kn[l], z(HEAD_PAD - NOPE_DIM)])),
        "pk": jnp.asarray(pk, BF16),
        "gkr": row1(jnp.concatenate([z(NOPE_DIM), gkr_, tailz])),
        "gkrsw": row1(jnp.concatenate([z(NOPE_DIM), gkr_[half:], gkr_[:half], tailz])),
        "wv_pair": wv_pair.astype(BF16),
        "wabs": wabs.astype(BF16), "wkb_s": wk.reshape(KV_RANK, -1).astype(BF16),
        "wv_s": wv.reshape(KV_RANK, -1).astype(BF16),
        "w_o_mla": w_o_mla[l].astype(BF16), "w_o_sb": w_o_sb[l].astype(BF16),
        "w_out": w_out[l].astype(BF16), "g_ffn": row1(g_ffn[l]),
        "w_router": jnp.concatenate([w_router[l], z(d, LANES - N_EXPERTS)], axis=1),
        "b_router": row1(jnp.concatenate([b_router[l], z(LANES - N_EXPERTS)])),
    }


def _emean(t_new):
    rows = MLA_HEADS * t_new
    r = np.arange(rows)[:, None] // t_new
    c = np.arange(MLA_HEADS * NOPE_DIM)[None, :] // NOPE_DIM
    return jnp.asarray((r == c) / NOPE_DIM, BF16)


def _rope_tables(pos):
    half = ROPE_DIM // 2
    inv = ROPE_THETA ** (-jnp.arange(half, dtype=F32) / half)
    ang = pos.astype(F32)[:, None] * inv[None, :]
    cos, sin = jnp.cos(ang), jnp.sin(ang)
    n = pos.shape[0]
    tail = jnp.zeros((n, LANES - MLA_QK_DIM), F32)
    cosb = jnp.concatenate([jnp.ones((n, NOPE_DIM), F32), cos, cos, tail], axis=1)
    sinb = jnp.concatenate([jnp.zeros((n, NOPE_DIM), F32), -sin, sin, tail], axis=1)
    return cosb, sinb


def _tri(n):
    j = np.arange(n)
    return jnp.asarray(j[:, None] >= j[None, :], BF16)


def kernel(x_prompt, x_sample, cache_mla_ckv, cache_mla_krope, cache_sb_k, cache_sb_v, page_table,
           c_prompt, c_sample, w_ada, b_ada, g_attn, w_in, g_qa, w_q_b, g_kva, w_kv_b, g_qn, g_qr,
           g_kn, g_kr, w_o_mla, w_o_sb, w_out, g_ffn, w_router, b_router, w_gu, b_gu, w_down, b_down):
    bsz, seq, d = x_prompt.shape
    nseq, t_new, _ = x_sample.shape
    depth = w_in.shape[0]
    n_pages = page_table.shape[1]
    page = cache_mla_ckv.shape[2]
    past_len = n_pages * page
    n_p, n_s = bsz * seq, nseq * t_new
    blk = min(ATT_BLK, seq)

    cos_p, sin_p = _rope_tables(jnp.arange(seq))
    cos_s, sin_s = _rope_tables(past_len + jnp.arange(t_new))
    cos_s, sin_s = jnp.tile(cos_s, (nseq, 1)), jnp.tile(sin_s, (nseq, 1))
    tri = _tri(blk)
    tri_s = _tri(2 * page)
    emean = _emean(t_new)
    cache_krt = cache_mla_krope.transpose(0, 1, 3, 2)
    cache_kt = cache_sb_k.transpose(0, 1, 3, 4, 2)
    cache_vt = cache_sb_v.transpose(0, 1, 3, 4, 2)
    tm_s = min(ROW_TILE, n_s)

    xp = x_prompt.reshape(n_p, d)
    xs = x_sample.reshape(n_s, d)
    c_all = jnp.concatenate([c_prompt, c_sample], axis=0)
    rows_p, rows_s = [], []
    for l in range(depth):
        w = _prep_weights(l, w_ada, b_ada, g_attn, w_in, g_qa, w_q_b, g_kva, w_kv_b, g_qn, g_qr, g_kn,
                          g_kr, w_o_mla, w_o_sb, w_out, g_ffn, w_router, b_router)
        w["emean"] = emean
        mod = _ada_call(c_all, w["w_ada"], w["b_ada"])
        mods_p = [mod[:bsz, i * d:(i + 1) * d].reshape(bsz, 1, d) for i in range(6)]
        mods_s = [jnp.repeat(mod[bsz:, i * d:(i + 1) * d], t_new, axis=0).reshape(n_s // tm_s, tm_s, d)
                  for i in range(6)]

        (ckv_p, kr_p, sbk_p, sbv_p, qm, km, ckvb, qs, ks, vs, gates_p) = _proj_call(
            xp, mods_p[0], mods_p[1], cos_p, sin_p, w, seq, BF16)
        r3 = lambda a: a.reshape(bsz, seq, a.shape[1])
        ym_p, ys_p = _prompt_attn_call(r3(qm), r3(km), r3(ckvb), r3(qs), r3(ks), r3(vs),
                                       w["wv_pair"], tri, blk)
        x1_p, f_p, ti_p, pr_p = _out_call(xp, ym_p.reshape(n_p, -1), ys_p.reshape(n_p, -1), gates_p,
                                          mods_p[2], mods_p[3], mods_p[4], w, seq)

        (ckv_s, kr_s, sbk_s, sbv_s, qm_s, km_s, _, qs_s, _, _, gates_s) = _proj_call(
            xs, mods_s[0], mods_s[1], cos_s, sin_s, w, None, F32)
        ym_s, ys_s = _sample_attn_call(page_table, qm_s, km_s, ckv_s, qs_s, sbk_s, sbv_s, w, tri_s,
                                       (cache_mla_ckv, cache_krt, cache_kt, cache_vt), l, t_new)
        x1_s, f_s, ti_s, pr_s = _out_call(xs, ym_s, ys_s, gates_s, mods_s[2], mods_s[3], mods_s[4], w, None)

        n = n_p + n_s
        f_all = jnp.concatenate([f_p, f_s, jnp.zeros((1, d), BF16)], axis=0)
        top_i = jnp.concatenate([ti_p[:, :TOP_K], ti_s[:, :TOP_K]], axis=0)
        tok, dest, blk_e, n_used = _route(top_i, n, MOE_ROWS)
        yb = _moe_call(blk_e, n_used, f_all[tok], w_gu[l], b_gu[l].reshape(N_EXPERTS, 1, -1),
                       w_down[l], b_down[l].reshape(N_EXPERTS, 1, -1))
        dest = dest.reshape(n, TOP_K)
        xp = _combine_call(x1_p, mods_p[5], pr_p, [yb[dest[:n_p, k]] for k in range(TOP_K)], seq)
        xs = _combine_call(x1_s, mods_s[5], pr_s, [yb[dest[n_p:, k]] for k in range(TOP_K)], None)

        rows_p.append((ckv_p, kr_p, sbk_p, sbv_p))
        rows_s.append((ckv_s, kr_s, sbk_s, sbv_s))

    stack_p = lambda i, tail: jnp.stack([r[i] for r in rows_p]).reshape((depth, bsz, seq) + tail)
    stack_s = lambda i, tail: jnp.stack([r[i] for r in rows_s]).reshape((depth, nseq, t_new) + tail)
    return (xp.reshape(bsz, seq, d), xs.reshape(nseq, t_new, d),
            stack_p(0, (KV_RANK,)), stack_p(1, (ROPE_DIM,)),
            stack_p(2, (SB_HEADS, SB_DIM)), stack_p(3, (SB_HEADS, SB_DIM)),
            stack_s(0, (KV_RANK,)), stack_s(1, (ROPE_DIM,)),
            stack_s(2, (SB_HEADS, SB_DIM)), stack_s(3, (SB_HEADS, SB_DIM)))
```

```python
import functools

import numpy as np
import jax
import jax.numpy as jnp
from jax import lax
from jax.experimental import pallas as pl
from jax.experimental.pallas import tpu as pltpu

F32 = jnp.float32
BF16 = jnp.bfloat16

MLA_HEADS = 8
Q_RANK = 256
KV_RANK = 128
NOPE_DIM = 64
ROPE_DIM = 32
V_DIM = 64
MLA_QK_DIM = NOPE_DIM + ROPE_DIM
MLA_SCALE = MLA_QK_DIM ** -0.5
ROPE_THETA = 10000.0
SB_HEADS = 8
SB_DIM = 64
SB_WIDTH = SB_HEADS * SB_DIM
SB_SCALE = SB_DIM ** -0.5
N_EXPERTS = 32
TOP_K = 4
SWIGLU_LIMIT = 7.0
SWIGLU_ALPHA = 1.702
EPS = 1e-6
NEG = -1e30
SB_DEAD = -3.0e38

LANES = 128
HEAD_PAD = 128
VMEM_LIMIT = 56 * 1024 * 1024

ROW_TILE = 256
ATT_BLK = 256
MLA_GROUP = 4
PAGES_PER_STEP = 16
MOE_ROWS = 512

C_QA = 0
C_CKV = C_QA + Q_RANK
C_KR = C_CKV + KV_RANK
C_KRSW = C_KR + LANES
C_QS = C_KRSW + LANES
C_KS = C_QS + SB_WIDTH
C_VS = C_KS + SB_WIDTH
C_GATE = C_VS + SB_WIDTH


def _nt(a, b):
    return lax.dot_general(a, b, (((1,), (1,)), ((), ())), preferred_element_type=F32)


def _mm(a, b):
    return jnp.dot(a, b, preferred_element_type=F32)


def _split_bf16(x):
    hi = x.astype(BF16)
    lo = (x - hi.astype(F32)).astype(BF16)
    return hi, lo


def _rms(x):
    return lax.rsqrt(jnp.mean(x * x, axis=-1, keepdims=True) + EPS)


def _cparams(n_axes):
    return pltpu.CompilerParams(dimension_semantics=("arbitrary",) * n_axes,
                                vmem_limit_bytes=VMEM_LIMIT)


def _ada_kernel(c_ref, w_ref, b_ref, o_ref):
    c = c_ref[...]
    s = c * jax.nn.sigmoid(c)
    o_ref[...] = _mm(s.astype(BF16), w_ref[...]) + b_ref[...]


def _ada_call(c, w, b):
    n, d = c.shape
    cols = w.shape[1]
    return pl.pallas_call(
        _ada_kernel,
        grid=(cols // d,),
        in_specs=[pl.BlockSpec((n, d), lambda j: (0, 0)),
                  pl.BlockSpec((d, d), lambda j: (0, j)),
                  pl.BlockSpec((1, d), lambda j: (0, j))],
        out_specs=pl.BlockSpec((n, d), lambda j: (0, j)),
        out_shape=jax.ShapeDtypeStruct((n, cols), F32),
        compiler_params=_cparams(1),
        name="adaln",
    )(c, w, b)


def _proj_kernel(x_ref, sh_ref, sc_ref, cos_ref, sin_ref, gattn_ref, win_ref, gqa_ref, wqb_ref,
                 gq_ref, gqsw_ref, pq_ref, gkva_ref, wkb_ref, gk_ref, pk_ref, gkr_ref, gkrsw_ref,
                 ckv_o, krope_o, sbk_o, sbv_o, qm_o, km_o, ckvb_o, qs_o, ks_o, vs_o, gates_o):
    x = x_ref[...]
    a = x * _rms(x) * gattn_ref[...] * (1.0 + sc_ref[0]) + sh_ref[0]
    h = _mm(a.astype(BF16), win_ref[...])
    cosb = cos_ref[...]
    sinb = sin_ref[...]

    qa = h[:, C_QA:C_QA + Q_RANK]
    qa_n = qa * _rms(qa) * gqa_ref[...]
    qq = _mm(qa_n.astype(BF16), wqb_ref[...])
    gq = gq_ref[...]
    gqsw = gqsw_ref[...]
    pq = pq_ref[...]
    width = MLA_HEADS * HEAD_PAD
    for hd in range(MLA_HEADS):
        lo_c = hd * HEAD_PAD
        q = qq[:, lo_c:lo_c + HEAD_PAD]
        qsw = qq[:, width + lo_c:width + lo_c + HEAD_PAD]
        hi, lo = _split_bf16(q * q)
        rin = lax.rsqrt(_mm(hi, pq) + _mm(lo, pq) + EPS)
        qm_o[:, lo_c:lo_c + HEAD_PAD] = ((q * gq * cosb + qsw * gqsw * sinb) * rin).astype(qm_o.dtype)

    ckv_raw = h[:, C_CKV:C_CKV + KV_RANK]
    ckv = ckv_raw * _rms(ckv_raw) * gkva_ref[...]
    ckv_o[...] = ckv
    ckvb = ckv.astype(BF16)
    ckvb_o[...] = ckvb
    kr = h[:, C_KR:C_KR + LANES]
    krsw = h[:, C_KRSW:C_KRSW + LANES]
    kr_rin = lax.rsqrt(jnp.sum(kr * kr, axis=-1, keepdims=True) * (1.0 / ROPE_DIM) + EPS)
    krope = (kr * gkr_ref[...] * cosb + krsw * gkrsw_ref[...] * sinb) * kr_rin
    krope_o[...] = krope[:, NOPE_DIM:NOPE_DIM + ROPE_DIM]

    kexp = _mm(ckvb, wkb_ref[...])
    gk = gk_ref[...]
    pk = pk_ref[...]
    for hd in range(MLA_HEADS):
        lo_c = hd * HEAD_PAD
        k = kexp[:, lo_c:lo_c + HEAD_PAD]
        hi, lo = _split_bf16(k * k)
        rin = lax.rsqrt(_mm(hi, pk) + _mm(lo, pk) + EPS)
        km_o[:, lo_c:lo_c + HEAD_PAD] = (k * gk * rin + krope).astype(km_o.dtype)

    qs_o[...] = (h[:, C_QS:C_QS + SB_WIDTH] * SB_SCALE).astype(qs_o.dtype)
    ks = h[:, C_KS:C_KS + SB_WIDTH]
    vs = h[:, C_VS:C_VS + SB_WIDTH]
    sbk_o[...] = ks
    sbv_o[...] = vs
    ks_o[...] = ks.astype(BF16)
    vs_o[...] = vs.astype(BF16)
    gates_o[...] = jax.nn.sigmoid(h[:, C_GATE:]).astype(BF16)


def _row_specs(rows, tm, per_seq_rows):
    if per_seq_rows is not None:
        tps = per_seq_rows // tm
        mod = lambda d: pl.BlockSpec((1, 1, d), lambda i: (i // tps, 0, 0))
        tab = pl.BlockSpec((tm, LANES), lambda i: (i % tps, 0))
    else:
        mod = lambda d: pl.BlockSpec((1, tm, d), lambda i: (i, 0, 0))
        tab = pl.BlockSpec((tm, LANES), lambda i: (i, 0))
    return mod, tab


def _proj_call(x, sh, sc, cosb, sinb, w, per_seq_rows, q_dtype):
    rows, d = x.shape
    tm = min(ROW_TILE, rows)
    mod, tab = _row_specs(rows, tm, per_seq_rows)
    full = lambda arr: pl.BlockSpec(arr.shape, lambda i: (0,) * arr.ndim)
    rowblk = lambda c: pl.BlockSpec((tm, c), lambda i: (i, 0))
    hw = MLA_HEADS * HEAD_PAD
    consts = [w["g_attn"], w["w_in"], w["g_qa"], w["w_qb"], w["gq"], w["gqsw"], w["pq"], w["g_kva"],
              w["w_kb"], w["gk"], w["pk"], w["gkr"], w["gkrsw"]]
    out_cols = [(KV_RANK, F32), (ROPE_DIM, F32), (SB_WIDTH, F32), (SB_WIDTH, F32), (hw, q_dtype),
                (hw, q_dtype), (KV_RANK, BF16), (SB_WIDTH, q_dtype), (SB_WIDTH, BF16), (SB_WIDTH, BF16),
                (2 * d, BF16)]
    return pl.pallas_call(
        _proj_kernel,
        grid=(rows // tm,),
        in_specs=[rowblk(d), mod(d), mod(d), tab, tab] + [full(c) for c in consts],
        out_specs=[rowblk(c) for c, _ in out_cols],
        out_shape=[jax.ShapeDtypeStruct((rows, c), dt) for c, dt in out_cols],
        compiler_params=_cparams(1),
        name="in_proj",
    )(x, sh, sc, cosb, sinb, *consts)


def _log1m(z):
    return jnp.minimum(-z, 0.0) - jnp.log(1.0 + jnp.exp(-jnp.abs(z)))


def _suffix_sums(log1m, tri):
    hi, lo = _split_bf16(log1m)
    return _mm(hi, tri) + _mm(lo, tri)


def _prompt_attn_kernel(qm_ref, km_ref, ckv_ref, qs_ref, ks_ref, vs_ref, wv_ref, tri_ref,
                        ym_ref, ys_ref, m_sc, l_sc, om_sc, ls_sc, os_sc, *, blk, n_blk):
    qi = pl.program_id(2)
    lane = lax.broadcasted_iota(jnp.int32, (blk, LANES), 1)
    tri = tri_ref[...]
    qs_pair = qs_ref[0]
    zero = jnp.zeros_like(qs_pair)
    half = LANES // 2
    heads = range(2)
    q_s = [jnp.where(lane < half, qs_pair, zero), jnp.where(lane >= half, qs_pair, zero)]
    q_m = [qm_ref[0, :, 0:HEAD_PAD], qm_ref[0, :, HEAD_PAD:2 * HEAD_PAD]]
    ones_v = jnp.ones((blk, LANES), BF16)

    def key_start(j):
        return pl.multiple_of(jnp.clip(j, 0, n_blk - 1) * blk, blk)

    def rel_pos(j):
        row = lax.broadcasted_iota(jnp.int32, (blk, blk), 0)
        col = lax.broadcasted_iota(jnp.int32, (blk, blk), 1)
        return col - row + (j - qi) * blk

    def mla_group(js, masked):
        k0s = [key_start(j) for j in js]
        vals = [jnp.concatenate([ckv_ref[0, pl.ds(k0, blk), :], ones_v], axis=1) for k0 in k0s]
        ss_h = [[_nt(q_m[hh], km_ref[0, pl.ds(k0, blk), hh * HEAD_PAD:(hh + 1) * HEAD_PAD]) for k0 in k0s]
                for hh in heads]
        if masked:
            causal = [rel_pos(j) <= 0 for j in js]
            ss_h = [[jnp.where(c, s, NEG) for c, s in zip(causal, ss)] for ss in ss_h]
        m_prevs = [m_sc[hh] for hh in heads]
        m_news = [jnp.maximum(m_prevs[hh], jnp.max(jnp.concatenate(ss_h[hh], axis=1), axis=-1, keepdims=True))
                  for hh in heads]
        for hh in heads:
            m_rep = jnp.concatenate([m_news[hh]] * (blk // LANES), axis=1)
            alpha = jnp.exp(m_prevs[hh] - m_news[hh])
            acc = jnp.concatenate([alpha * om_sc[hh], alpha * l_sc[hh]], axis=1)
            for idx, (s, val) in enumerate(zip(ss_h[hh], vals)):
                p = jnp.exp(s - m_rep)
                if masked:
                    p = jnp.where(causal[idx], p, 0.0)
                acc = acc + _mm(p.astype(BF16), val)
            om_sc[hh] = acc[:, 0:KV_RANK]
            l_sc[hh] = acc[:, KV_RANK:]
            m_sc[hh] = m_news[hh]

    def sb_group(js, masked):
        k0s = [key_start(j) for j in js]
        kss = [ks_ref[0, pl.ds(k0, blk), :] for k0 in k0s]
        vss = [vs_ref[0, pl.ds(k0, blk), :] for k0 in k0s]
        zs = [[_nt(q_s[hh], ks) for ks in kss] for hh in heads]
        l1 = [[_log1m(z) for z in zs[hh]] for hh in heads]
        if masked:
            strict = [jnp.logical_and(rel_pos(j) < 0, j >= 0) for j in js]
            l1 = [[jnp.where(st, v, 0.0) for st, v in zip(strict, l1[hh])] for hh in heads]
        cs = [[_suffix_sums(v, tri) for v in l1[hh]] for hh in heads]
        for hh in heads:
            ls = ls_sc[hh]
            acc = os_sc[hh]
            for idx in range(len(js)):
                a = jnp.exp(zs[hh][idx] + cs[hh][idx] + ls)
                if masked:
                    a = jnp.where(strict[idx], a, 0.0)
                acc = acc + _mm(a.astype(BF16), vss[idx])
                ls = ls + cs[hh][idx][:, 0:1]
            os_sc[hh] = acc
            ls_sc[hh] = ls

    for hh in heads:
        m_sc[hh] = jnp.full((blk, LANES), NEG, F32)
        l_sc[hh] = jnp.zeros((blk, LANES), F32)
        om_sc[hh] = jnp.zeros((blk, KV_RANK), F32)
        ls_sc[hh] = jnp.zeros((blk, 1), F32)
        os_sc[hh] = jnp.zeros((blk, LANES), F32)

    n_main = qi // MLA_GROUP
    mla_group([n_main * MLA_GROUP + u for u in range(MLA_GROUP)], True)

    def mla_body(it, carry):
        mla_group([it * MLA_GROUP + u for u in range(MLA_GROUP)], False)
        return carry

    lax.fori_loop(0, n_main, mla_body, 0)

    sb_group([qi, qi - 1], True)

    def sb_alive():
        return jnp.max(jnp.maximum(ls_sc[0], ls_sc[1])) > SB_DEAD

    def sb_cond(carry):
        j, alive = carry
        return jnp.logical_and(j >= 0, alive)

    def sb_body(carry):
        j, _ = carry
        sb_group([j], False)
        return j - 1, sb_alive()

    lax.while_loop(sb_cond, sb_body, (qi - 2, sb_alive()))

    ylat = [(om_sc[hh] / l_sc[hh]).astype(BF16) for hh in heads]
    ym_ref[0] = _mm(jnp.concatenate(ylat, axis=1), wv_ref[0]).astype(BF16)
    ys_ref[0] = jnp.where(lane < half, os_sc[0], os_sc[1]).astype(BF16)


def _prompt_attn_call(qm, km, ckvb, qs, ks, vs, wv_pair, tri, blk):
    b, s, _ = qm.shape
    hp = MLA_HEADS // 2
    qspec = lambda c: pl.BlockSpec((1, blk, c), lambda bi, h, qi: (bi, qi, h))
    kspec = lambda c: pl.BlockSpec((1, s, c), lambda bi, h, qi: (bi, 0, h))
    return pl.pallas_call(
        functools.partial(_prompt_attn_kernel, blk=blk, n_blk=s // blk),
        grid=(b, hp, s // blk),
        in_specs=[qspec(2 * HEAD_PAD), kspec(2 * HEAD_PAD),
                  pl.BlockSpec((1, s, KV_RANK), lambda bi, h, qi: (bi, 0, 0)),
                  qspec(LANES), kspec(LANES), kspec(LANES),
                  pl.BlockSpec((1, 2 * KV_RANK, LANES), lambda bi, h, qi: (h, 0, 0)),
                  pl.BlockSpec((blk, blk), lambda bi, h, qi: (0, 0))],
        out_specs=[qspec(LANES), qspec(LANES)],
        out_shape=[jax.ShapeDtypeStruct((b, s, MLA_HEADS * V_DIM), BF16),
                   jax.ShapeDtypeStruct((b, s, SB_WIDTH), BF16)],
        scratch_shapes=[pltpu.VMEM((2, blk, LANES), F32), pltpu.VMEM((2, blk, LANES), F32),
                        pltpu.VMEM((2, blk, KV_RANK), F32), pltpu.VMEM((2, blk, 1), F32),
                        pltpu.VMEM((2, blk, LANES), F32)],
        compiler_params=_cparams(3),
        name="prompt_attn",
    )(qm, km, ckvb, qs, ks, vs, wv_pair, tri)


def _sample_attn_kernel(pt_ref, qm_ref, kmn_ref, ckvn_ref, qs_ref, ksn_ref, vsn_ref,
                        wabs_ref, wkb_ref, wv_ref, emean_ref, tri_ref, *rest,
                        t_new, page, n_grp, n_pages, nseq, layer):
    n_pg = PAGES_PER_STEP
    ckv_refs = rest[0:n_pg]
    kr_refs = rest[n_pg:2 * n_pg]
    k_hbm, v_hbm = rest[2 * n_pg:2 * n_pg + 2]
    ym_ref, ys_ref = rest[2 * n_pg + 2:2 * n_pg + 4]
    (qabs_sc, qr_sc, qbd_sc, m_sc, l_sc, om_sc, ls_sc, os_sc, kbuf, vbuf, sem) = rest[2 * n_pg + 4:]

    b = pl.program_id(0)
    p = pl.program_id(1)
    slot = b % 2
    rows = MLA_HEADS * t_new
    hw = MLA_HEADS * HEAD_PAD
    tri = tri_ref[...]

    def head_mask(cols, per_head):
        r = lax.broadcasted_iota(jnp.int32, (rows, cols), 0) // t_new
        c = lax.broadcasted_iota(jnp.int32, (rows, cols), 1) // per_head
        return r == c

    def sb_copies(seq, grp, sl):
        out = []
        for j in range(n_pg):
            phys = pt_ref[seq * n_pages + (n_pages - 1 - (grp * n_pg + j))]
            out.append(pltpu.make_async_copy(k_hbm.at[layer, phys], kbuf.at[sl, j], sem.at[0, sl]))
            out.append(pltpu.make_async_copy(v_hbm.at[layer, phys], vbuf.at[sl, j], sem.at[1, sl]))
        return out

    def sb_group(sl):
        for cidx in range(n_pg // 2):
            newer, older = 2 * cidx, 2 * cidx + 1

            @pl.when(jnp.max(ls_sc[...]) > SB_DEAD)
            def _chunk():
                flat = lambda buf, j: buf[sl, j].reshape(SB_WIDTH, page).astype(BF16)
                kt = jnp.concatenate([flat(kbuf, older), flat(kbuf, newer)], axis=1)
                vt = jnp.concatenate([flat(vbuf, older), flat(vbuf, newer)], axis=1)
                z = _mm(qbd_sc[...], kt)
                c = _suffix_sums(_log1m(z), tri)
                ls = ls_sc[...]
                a = jnp.exp(z + c + ls).astype(BF16)
                os_sc[...] = os_sc[...] + _nt(a, vt)
                ls_sc[...] = ls + c[:, 0:1]

    @pl.when(p == 0)
    def _init():
        @pl.when(b == 0)
        def _():
            for cp in sb_copies(0, 0, 0):
                cp.start()

        @pl.when(b + 1 < nseq)
        def _():
            for cp in sb_copies(b + 1, 0, 1 - slot):
                cp.start()

        pad = page - t_new
        padk = lambda ref: jnp.concatenate(
            [ref[...], jnp.zeros((pad, ref.shape[1]), F32)], axis=0).astype(BF16)
        qm = jnp.concatenate([qm_ref[...]] * MLA_HEADS, axis=0)
        qbd_m = jnp.where(head_mask(hw, HEAD_PAD), qm, 0.0).astype(BF16)
        qabs_sc[...] = _mm(qbd_m, wabs_ref[...]).astype(BF16)
        qr = [qm_ref[:, hd * HEAD_PAD + NOPE_DIM:hd * HEAD_PAD + NOPE_DIM + ROPE_DIM]
              for hd in range(MLA_HEADS)]
        qr_sc[...] = jnp.concatenate(qr, axis=0).astype(BF16)
        qs = jnp.concatenate([qs_ref[...]] * SB_HEADS, axis=0)
        qbd_s = jnp.where(head_mask(SB_WIDTH, SB_DIM), qs, 0.0).astype(BF16)
        qbd_sc[...] = qbd_s

        qrow = lax.broadcasted_iota(jnp.int32, (rows, page), 0) % t_new
        kcol = lax.broadcasted_iota(jnp.int32, (rows, page), 1)
        causal = kcol <= qrow
        strict = kcol < qrow
        s = jnp.where(causal, _nt(qbd_m, padk(kmn_ref)), NEG)
        m_new = jnp.broadcast_to(jnp.max(s, axis=-1, keepdims=True), (rows, LANES))
        pr = jnp.where(causal, jnp.exp(s - m_new), 0.0)
        pv = _mm(pr.astype(BF16), jnp.concatenate([padk(ckvn_ref), jnp.ones((page, LANES), BF16)], axis=1))
        m_sc[...] = m_new
        om_sc[...] = pv[:, 0:KV_RANK]
        l_sc[...] = pv[:, KV_RANK:]
        z = _nt(qbd_s, padk(ksn_ref))
        log1m = jnp.where(strict, _log1m(z), 0.0)
        c = _suffix_sums(log1m, tri[0:page, 0:page])
        a = jnp.where(strict, jnp.exp(z + c), 0.0)
        os_sc[...] = _mm(a.astype(BF16), padk(vsn_ref))
        ls_sc[...] = c[:, 0:1]

        for cp in sb_copies(b, 0, slot):
            cp.wait()
        sb_group(slot)

    @pl.when(jnp.logical_and(p > 0, jnp.max(ls_sc[...]) > SB_DEAD))
    def _older():
        for cp in sb_copies(b, p, slot):
            cp.start()
        for cp in sb_copies(b, p, slot):
            cp.wait()
        sb_group(slot)

    qabs = qabs_sc[...]
    qr = qr_sc[...]
    wkb = wkb_ref[...]
    emean = emean_ref[...]
    ones_v = jnp.ones((2 * page, LANES), BF16)
    chunks = [(2 * c + 1, 2 * c) for c in range(n_pg // 2)]
    ckvs = [jnp.concatenate([ckv_refs[o][0, 0], ckv_refs[n][0, 0]], axis=0).astype(BF16) for o, n in chunks]
    krts = [jnp.concatenate([kr_refs[o][0, 0], kr_refs[n][0, 0]], axis=1).astype(BF16) for o, n in chunks]
    kexps = [_mm(ckv, wkb) for ckv in ckvs]
    sss = [_nt(emean, (k * k).astype(BF16)) for k in kexps]
    ss = [_nt(qabs, ckv) * lax.rsqrt(msq + EPS) + _mm(qr, krt) for ckv, msq, krt in zip(ckvs, sss, krts)]
    m_prev = m_sc[...]
    m_new = jnp.maximum(m_prev, jnp.max(jnp.concatenate(ss, axis=1), axis=-1, keepdims=True))
    m_rep = jnp.concatenate([m_new] * (2 * page // LANES), axis=1)
    alpha = jnp.exp(m_prev - m_new)
    acc = jnp.concatenate([alpha * om_sc[...], alpha * l_sc[...]], axis=1)
    for s_c, ckv in zip(ss, ckvs):
        pr = jnp.exp(s_c - m_rep)
        acc = acc + _mm(pr.astype(BF16), jnp.concatenate([ckv, ones_v], axis=1))
    om_sc[...] = acc[:, 0:KV_RANK]
    l_sc[...] = acc[:, KV_RANK:]
    m_sc[...] = m_new

    @pl.when(p == n_grp - 1)
    def _fin():
        ylat = (om_sc[...] / l_sc[...]).astype(BF16)
        full_m = jnp.where(head_mask(MLA_HEADS * V_DIM, V_DIM), _mm(ylat, wv_ref[...]), 0.0)
        full_s = jnp.where(head_mask(SB_WIDTH, SB_DIM), os_sc[...], 0.0)
        ym = full_m[0:t_new]
        ys = full_s[0:t_new]
        for hd in range(1, MLA_HEADS):
            ym = ym + full_m[hd * t_new:(hd + 1) * t_new]
            ys = ys + full_s[hd * t_new:(hd + 1) * t_new]
        ym_ref[...] = ym.astype(BF16)
        ys_ref[...] = ys.astype(BF16)


def _sample_attn_call(page_table, qm, kmn, ckvn, qs, ksn, vsn, w, tri, caches, layer, t_new):
    cache_ckv, cache_krt, cache_kt, cache_vt = caches
    nseq, n_pages = page_table.shape
    page = cache_ckv.shape[2]
    n_pg = PAGES_PER_STEP
    n_grp = n_pages // n_pg
    rows = MLA_HEADS * t_new
    pt = page_table.reshape(-1)

    seqblk = lambda c: pl.BlockSpec((t_new, c), lambda b, p, pt_r: (b, 0))
    full = lambda arr: pl.BlockSpec(arr.shape, lambda b, p, pt_r: (0,) * arr.ndim)

    def page_spec(j, shape):
        def imap(b, p, pt_r):
            return (layer, pt_r[b * n_pages + (n_pages - 1 - (p * n_pg + j))], 0, 0)
        return pl.BlockSpec((1, 1) + shape, imap)

    consts = [w["wabs"], w["wkb_s"], w["wv_s"], w["emean"], tri]
    page_specs = ([page_spec(j, (page, KV_RANK)) for j in range(n_pg)]
                  + [page_spec(j, (ROPE_DIM, page)) for j in range(n_pg)])
    page_args = [cache_ckv] * n_pg + [cache_krt] * n_pg
    hbm = pl.BlockSpec(memory_space=pl.ANY)
    hw = MLA_HEADS * HEAD_PAD
    sb_buf = pltpu.VMEM((2, n_pg, SB_HEADS, SB_DIM, page), F32)
    grid_spec = pltpu.PrefetchScalarGridSpec(
        num_scalar_prefetch=1,
        grid=(nseq, n_grp),
        in_specs=[seqblk(hw), seqblk(hw), seqblk(KV_RANK), seqblk(SB_WIDTH), seqblk(SB_WIDTH),
                  seqblk(SB_WIDTH)] + [full(c) for c in consts] + page_specs + [hbm, hbm],
        out_specs=[seqblk(MLA_HEADS * V_DIM), seqblk(SB_WIDTH)],
        scratch_shapes=[pltpu.VMEM((rows, KV_RANK), BF16), pltpu.VMEM((rows, ROPE_DIM), BF16),
                        pltpu.VMEM((rows, SB_WIDTH), BF16),
                        pltpu.VMEM((rows, LANES), F32), pltpu.VMEM((rows, LANES), F32),
                        pltpu.VMEM((rows, KV_RANK), F32), pltpu.VMEM((rows, 1), F32),
                        pltpu.VMEM((rows, SB_WIDTH), F32), sb_buf, sb_buf,
                        pltpu.SemaphoreType.DMA((2, 2))],
    )
    return pl.pallas_call(
        functools.partial(_sample_attn_kernel, t_new=t_new, page=page, n_grp=n_grp, n_pages=n_pages,
                          nseq=nseq, layer=layer),
        grid_spec=grid_spec,
        out_shape=[jax.ShapeDtypeStruct((nseq * t_new, MLA_HEADS * V_DIM), BF16),
                   jax.ShapeDtypeStruct((nseq * t_new, SB_WIDTH), BF16)],
        compiler_params=_cparams(2),
        name="sample_attn",
    )(pt, qm, kmn, ckvn, qs, ksn, vsn, *consts, *page_args, cache_kt, cache_vt)


def _out_kernel(x_ref, ym_ref, ys_ref, gates_ref, gt1_ref, sh2_ref, sc2_ref, woa_ref, wob_ref, wout_ref,
                gffn_ref, wr_ref, br_ref, x1_o, f_o, topi_o, prob_o):
    d = x_ref.shape[1]
    ga = gates_ref[:, 0:d].astype(F32)
    gb = gates_ref[:, d:2 * d].astype(F32)
    mix = ga * _mm(ym_ref[...], woa_ref[...]) + gb * _mm(ys_ref[...], wob_ref[...])
    x1 = x_ref[...] + gt1_ref[0] * _mm(mix.astype(BF16), wout_ref[...])
    x1_o[...] = x1
    f = x1 * _rms(x1) * gffn_ref[...] * (1.0 + sc2_ref[0]) + sh2_ref[0]
    f_o[...] = f.astype(BF16)

    logits = jnp.dot(f, wr_ref[...], preferred_element_type=F32,
                     precision=lax.Precision.HIGHEST) + br_ref[...]
    lane = lax.broadcasted_iota(jnp.int32, logits.shape, 1)
    lanef = lane.astype(F32)
    neg_inf = jnp.float32(-jnp.inf)
    work = jnp.where(lane < N_EXPERTS, logits, neg_inf)
    topv = []
    topi = jnp.zeros(logits.shape, F32)
    for k in range(TOP_K):
        mx = jnp.max(work, axis=-1, keepdims=True)
        idx = jnp.min(jnp.where(work == mx, lanef, float(LANES)), axis=-1, keepdims=True)
        topv.append(mx)
        topi = jnp.where(lane == k, idx, topi)
        work = jnp.where(lanef == idx, neg_inf, work)
    ex = [jnp.exp(v - topv[0]) for v in topv]
    den = ex[0]
    for e in ex[1:]:
        den = den + e
    prob = jnp.zeros(logits.shape, F32)
    for k in range(TOP_K):
        prob = jnp.where(lane == k, ex[k] / den, prob)
    topi_o[...] = topi.astype(jnp.int32)
    prob_o[...] = prob


def _out_call(x, ym, ys, gates, gt1, sh2, sc2, w, per_seq_rows):
    rows, d = x.shape
    tm = min(ROW_TILE, rows)
    mod, _ = _row_specs(rows, tm, per_seq_rows)
    full = lambda arr: pl.BlockSpec(arr.shape, lambda i: (0,) * arr.ndim)
    rowblk = lambda c: pl.BlockSpec((tm, c), lambda i: (i, 0))
    consts = [w["w_o_mla"], w["w_o_sb"], w["w_out"], w["g_ffn"], w["w_router"], w["b_router"]]
    return pl.pallas_call(
        _out_kernel,
        grid=(rows // tm,),
        in_specs=[rowblk(d), rowblk(ym.shape[1]), rowblk(ys.shape[1]), rowblk(2 * d),
                  mod(d), mod(d), mod(d)] + [full(c) for c in consts],
        out_specs=[rowblk(d), rowblk(d), rowblk(LANES), rowblk(LANES)],
        out_shape=[jax.ShapeDtypeStruct((rows, d), F32), jax.ShapeDtypeStruct((rows, d), BF16),
                   jax.ShapeDtypeStruct((rows, LANES), jnp.int32),
                   jax.ShapeDtypeStruct((rows, LANES), F32)],
        compiler_params=_cparams(1),
        name="out_proj",
    )(x, ym, ys, gates, gt1, sh2, sc2, *consts)


def _moe_kernel(be_ref, nu_ref, x_ref, wgu_ref, bgu_ref, wd_ref, bd_ref, o_ref, wgu_bf, wd_bf):
    i = pl.program_id(0)
    d_ff = wd_ref.shape[1]

    @pl.when(jnp.logical_or(i == 0, be_ref[i] != be_ref[jnp.maximum(i - 1, 0)]))
    def _cast():
        wgu_bf[...] = wgu_ref[0].astype(BF16)
        wd_bf[...] = wd_ref[0].astype(BF16)

    @pl.when(i < nu_ref[0])
    def _compute():
        gu = _mm(x_ref[...], wgu_bf[...]) + bgu_ref[0]
        gate = jnp.minimum(gu[:, :d_ff], SWIGLU_LIMIT)
        up = jnp.clip(gu[:, d_ff:], -SWIGLU_LIMIT, SWIGLU_LIMIT)
        act = (up + 1.0) * gate * jax.nn.sigmoid(SWIGLU_ALPHA * gate)
        o_ref[...] = (_mm(act.astype(BF16), wd_bf[...]) + bd_ref[0]).astype(o_ref.dtype)

    @pl.when(i >= nu_ref[0])
    def _unused():
        o_ref[...] = jnp.zeros(o_ref.shape, o_ref.dtype)


def _moe_call(blk_e, n_used, xb, w_gu, b_gu, w_down, b_down):
    n_rows, d = xb.shape
    mb = MOE_ROWS
    d_ff = w_down.shape[1]
    grid_spec = pltpu.PrefetchScalarGridSpec(
        num_scalar_prefetch=2,
        grid=(n_rows // mb,),
        in_specs=[pl.BlockSpec((mb, d), lambda i, be, nu: (i, 0)),
                  pl.BlockSpec((1, d, 2 * d_ff), lambda i, be, nu: (be[i], 0, 0)),
                  pl.BlockSpec((1, 1, 2 * d_ff), lambda i, be, nu: (be[i], 0, 0)),
                  pl.BlockSpec((1, d_ff, d), lambda i, be, nu: (be[i], 0, 0)),
                  pl.BlockSpec((1, 1, d), lambda i, be, nu: (be[i], 0, 0))],
        out_specs=pl.BlockSpec((mb, d), lambda i, be, nu: (i, 0)),
        scratch_shapes=[pltpu.VMEM((d, 2 * d_ff), BF16), pltpu.VMEM((d_ff, d), BF16)],
    )
    return pl.pallas_call(
        _moe_kernel,
        grid_spec=grid_spec,
        out_shape=jax.ShapeDtypeStruct((n_rows, d), BF16),
        compiler_params=_cparams(1),
        name="moe_mlp",
    )(blk_e, n_used, xb, w_gu, b_gu, w_down, b_down)


def _route(top_i, n, mb):
    nk = n * TOP_K
    flat_e = top_i.reshape(-1)
    order = jnp.argsort(flat_e).astype(jnp.int32)
    rank = jnp.argsort(order).astype(jnp.int32)
    experts = jnp.arange(N_EXPERTS, dtype=jnp.int32)
    counts = jnp.sum((flat_e[:, None] == experts[None, :]).astype(jnp.int32), axis=0)
    padded = (counts + mb - 1) // mb * mb
    start = jnp.cumsum(counts) - counts
    pend = jnp.cumsum(padded)
    pstart = pend - padded
    dest = pstart[flat_e] + rank - start[flat_e]
    n_blocks = -(-nk // mb) + N_EXPERTS
    blk_e = jnp.minimum(jnp.sum((pend[None, :] <= (jnp.arange(n_blocks, dtype=jnp.int32) * mb)[:, None])
                                .astype(jnp.int32), axis=1), N_EXPERTS - 1)
    row = jnp.arange(n_blocks * mb, dtype=jnp.int32)
    row_e = jnp.repeat(blk_e, mb)
    within = row - pstart[row_e]
    src = jnp.clip(start[row_e] + within, 0, nk - 1)
    tok = jnp.where(within < counts[row_e], order[src] // TOP_K, n).astype(jnp.int32)
    n_used = (pend[-1] // mb).astype(jnp.int32).reshape(1)
    return tok, dest.astype(jnp.int32), blk_e.astype(jnp.int32), n_used


def _combine_kernel(x1_ref, gt2_ref, prob_ref, *rest):
    yk_refs, o_ref = rest[:TOP_K], rest[TOP_K]
    prob = prob_ref[...]
    ff = prob[:, 0:1] * yk_refs[0][...].astype(F32)
    for k in range(1, TOP_K):
        ff = ff + prob[:, k:k + 1] * yk_refs[k][...].astype(F32)
    o_ref[...] = x1_ref[...] + gt2_ref[0] * ff


def _combine_call(x1, gt2, prob, yks, per_seq_rows):
    rows, d = x1.shape
    tm = min(ROW_TILE, rows)
    mod, _ = _row_specs(rows, tm, per_seq_rows)
    rowblk = lambda c: pl.BlockSpec((tm, c), lambda i: (i, 0))
    return pl.pallas_call(
        _combine_kernel,
        grid=(rows // tm,),
        in_specs=[rowblk(d), mod(d), rowblk(LANES)] + [rowblk(d)] * TOP_K,
        out_specs=rowblk(d),
        out_shape=jax.ShapeDtypeStruct((rows, d), F32),
        compiler_params=_cparams(1),
        name="combine",
    )(x1, gt2, prob, *yks)


def _prep_weights(l, w_ada, b_ada, g_attn, w_in, g_qa, w_q_b, g_kva, w_kv_b, g_qn, g_qr, g_kn, g_kr,
                  w_o_mla, w_o_sb, w_out, g_ffn, w_router, b_router):
    d = w_in.shape[1]
    half = ROPE_DIM // 2
    z = lambda *s: jnp.zeros(s, F32)
    wi = w_in[l]
    o = 0
    cuts = []
    for width in (Q_RANK, KV_RANK, ROPE_DIM, SB_WIDTH, SB_WIDTH, SB_WIDTH, d, d):
        cuts.append(wi[:, o:o + width])
        o += width
    c_qa, c_ckv, c_kr, c_qs, c_ks, c_vs, c_ga, c_gb = cuts
    tail = LANES - NOPE_DIM - ROPE_DIM
    kr_blk = jnp.concatenate([z(d, NOPE_DIM), c_kr, z(d, tail)], axis=1)
    krsw_blk = jnp.concatenate([z(d, NOPE_DIM), c_kr[:, half:], c_kr[:, :half], z(d, tail)], axis=1)
    w_in2 = jnp.concatenate([c_qa, c_ckv, kr_blk, krsw_blk, c_qs, c_ks, c_vs, c_ga, c_gb], axis=1)

    wq = w_q_b[l].reshape(Q_RANK, MLA_HEADS, MLA_QK_DIM)
    wq_main = jnp.concatenate([wq, z(Q_RANK, MLA_HEADS, tail)], axis=-1)
    wq_sw = jnp.concatenate([z(Q_RANK, MLA_HEADS, NOPE_DIM), wq[:, :, NOPE_DIM + half:],
                             wq[:, :, NOPE_DIM:NOPE_DIM + half], z(Q_RANK, MLA_HEADS, tail)], axis=-1)
    w_qb = jnp.concatenate([wq_main.reshape(Q_RANK, -1), wq_sw.reshape(Q_RANK, -1)], axis=1)

    wkv = w_kv_b[l].reshape(KV_RANK, MLA_HEADS, NOPE_DIM + V_DIM)
    wk = wkv[:, :, :NOPE_DIM]
    wv = wkv[:, :, NOPE_DIM:]
    w_kb = jnp.concatenate([wk, z(KV_RANK, MLA_HEADS, HEAD_PAD - NOPE_DIM)], axis=-1).reshape(KV_RANK, -1)
    wv_t = wv.transpose(1, 0, 2)
    zb = z(MLA_HEADS // 2, KV_RANK, V_DIM)
    wv_pair = jnp.concatenate([jnp.concatenate([wv_t[0::2], zb], axis=2),
                               jnp.concatenate([zb, wv_t[1::2]], axis=2)], axis=1)
    wabs = jnp.concatenate([(wk * g_kn[l][None, None, :]).transpose(1, 2, 0),
                            z(MLA_HEADS, HEAD_PAD - NOPE_DIM, KV_RANK)], axis=1).reshape(-1, KV_RANK)

    lane = np.arange(LANES)
    grp_n = lane < NOPE_DIM
    grp_r = (lane >= NOPE_DIM) & (lane < MLA_QK_DIM)
    pq = (np.outer(grp_n, grp_n) / NOPE_DIM + np.outer(grp_r, grp_r) / ROPE_DIM).astype(np.float32)
    pk = (np.outer(grp_n, np.ones(LANES, bool)) / NOPE_DIM).astype(np.float32)
    tailz = z(tail)
    row1 = lambda v: v.reshape(1, -1)
    gqr, gkr_ = g_qr[l], g_kr[l]
    return {
        "w_ada": w_ada[l].astype(BF16), "b_ada": row1(b_ada[l]),
        "g_attn": row1(g_attn[l]), "w_in": w_in2.astype(BF16), "g_qa": row1(g_qa[l]),
        "w_qb": w_qb.astype(BF16),
        "gq": row1(jnp.concatenate([g_qn[l], gqr, tailz]) * MLA_SCALE),
        "gqsw": row1(jnp.concatenate([z(NOPE_DIM), gqr[half:], gqr[:half], tailz]) * MLA_SCALE),
        "pq": jnp.asarray(pq, BF16), "g_kva": row1(g_kva[l]), "w_kb": w_kb.astype(BF16),
        "gk": row1(jnp.concatenate([g_kn[l], z(HEAD_PAD - NOPE_DIM)])),
        "pk": jnp.asarray(pk, BF16),
        "gkr": row1(jnp.concatenate([z(NOPE_DIM), gkr_, tailz])),
        "gkrsw": row1(jnp.concatenate([z(NOPE_DIM), gkr_[half:], gkr_[:half], tailz])),
        "wv_pair": wv_pair.astype(BF16),
        "wabs": wabs.astype(BF16), "wkb_s": wk.reshape(KV_RANK, -1).astype(BF16),
        "wv_s": wv.reshape(KV_RANK, -1).astype(BF16),
        "w_o_mla": w_o_mla[l].astype(BF16), "w_o_sb": w_o_sb[l].astype(BF16),
        "w_out": w_out[l].astype(BF16), "g_ffn": row1(g_ffn[l]),
        "w_router": jnp.concatenate([w_router[l], z(d, LANES - N_EXPERTS)], axis=1),
        "b_router": row1(jnp.concatenate([b_router[l], z(LANES - N_EXPERTS)])),
    }


def _emean(t_new):
    rows = MLA_HEADS * t_new
    r = np.arange(rows)[:, None] // t_new
    c = np.arange(MLA_HEADS * NOPE_DIM)[None, :] // NOPE_DIM
    return jnp.asarray((r == c) / NOPE_DIM, BF16)


def _rope_tables(pos):
    half = ROPE_DIM // 2
    inv = ROPE_THETA ** (-jnp.arange(half, dtype=F32) / half)
    ang = pos.astype(F32)[:, None] * inv[None, :]
    cos, sin = jnp.cos(ang), jnp.sin(ang)
    n = pos.shape[0]
    tail = jnp.zeros((n, LANES - MLA_QK_DIM), F32)
    cosb = jnp.concatenate([jnp.ones((n, NOPE_DIM), F32), cos, cos, tail], axis=1)
    sinb = jnp.concatenate([jnp.zeros((n, NOPE_DIM), F32), -sin, sin, tail], axis=1)
    return cosb, sinb


def _tri(n):
    j = np.arange(n)
    return jnp.asarray(j[:, None] >= j[None, :], BF16)


def kernel(x_prompt, x_sample, cache_mla_ckv, cache_mla_krope, cache_sb_k, cache_sb_v, page_table,
           c_prompt, c_sample, w_ada, b_ada, g_attn, w_in, g_qa, w_q_b, g_kva, w_kv_b, g_qn, g_qr,
           g_kn, g_kr, w_o_mla, w_o_sb, w_out, g_ffn, w_router, b_router, w_gu, b_gu, w_down, b_down):
    bsz, seq, d = x_prompt.shape
    nseq, t_new, _ = x_sample.shape
    depth = w_in.shape[0]
    n_pages = page_table.shape[1]
    page = cache_mla_ckv.shape[2]
    past_len = n_pages * page
    n_p, n_s = bsz * seq, nseq * t_new
    blk = min(ATT_BLK, seq)

    cos_p, sin_p = _rope_tables(jnp.arange(seq))
    cos_s, sin_s = _rope_tables(past_len + jnp.arange(t_new))
    cos_s, sin_s = jnp.tile(cos_s, (nseq, 1)), jnp.tile(sin_s, (nseq, 1))
    tri = _tri(blk)
    tri_s = _tri(2 * page)
    emean = _emean(t_new)
    cache_krt = cache_mla_krope.transpose(0, 1, 3, 2)
    cache_kt = cache_sb_k.transpose(0, 1, 3, 4, 2)
    cache_vt = cache_sb_v.transpose(0, 1, 3, 4, 2)
    tm_s = min(ROW_TILE, n_s)

    xp = x_prompt.reshape(n_p, d)
    xs = x_sample.reshape(n_s, d)
    c_all = jnp.concatenate([c_prompt, c_sample], axis=0)
    rows_p, rows_s = [], []
    for l in range(depth):
        w = _prep_weights(l, w_ada, b_ada, g_attn, w_in, g_qa, w_q_b, g_kva, w_kv_b, g_qn, g_qr, g_kn,
                          g_kr, w_o_mla, w_o_sb, w_out, g_ffn, w_router, b_router)
        w["emean"] = emean
        mod = _ada_call(c_all, w["w_ada"], w["b_ada"])
        mods_p = [mod[:bsz, i * d:(i + 1) * d].reshape(bsz, 1, d) for i in range(6)]
        mods_s = [jnp.repeat(mod[bsz:, i * d:(i + 1) * d], t_new, axis=0).reshape(n_s // tm_s, tm_s, d)
                  for i in range(6)]

        (ckv_p, kr_p, sbk_p, sbv_p, qm, km, ckvb, qs, ks, vs, gates_p) = _proj_call(
            xp, mods_p[0], mods_p[1], cos_p, sin_p, w, seq, BF16)
        r3 = lambda a: a.reshape(bsz, seq, a.shape[1])
        ym_p, ys_p = _prompt_attn_call(r3(qm), r3(km), r3(ckvb), r3(qs), r3(ks), r3(vs),
                                       w["wv_pair"], tri, blk)
        x1_p, f_p, ti_p, pr_p = _out_call(xp, ym_p.reshape(n_p, -1), ys_p.reshape(n_p, -1), gates_p,
                                          mods_p[2], mods_p[3], mods_p[4], w, seq)

        (ckv_s, kr_s, sbk_s, sbv_s, qm_s, km_s, _, qs_s, _, _, gates_s) = _proj_call(
            xs, mods_s[0], mods_s[1], cos_s, sin_s, w, None, F32)
        ym_s, ys_s = _sample_attn_call(page_table, qm_s, km_s, ckv_s, qs_s, sbk_s, sbv_s, w, tri_s,
                                       (cache_mla_ckv, cache_krt, cache_kt, cache_vt), l, t_new)
        x1_s, f_s, ti_s, pr_s = _out_call(xs, ym_s, ys_s, gates_s, mods_s[2], mods_s[3], mods_s[4], w, None)

        n = n_p + n_s
        f_all = jnp.concatenate([f_p, f_s, jnp.zeros((1, d), BF16)], axis=0)
        top_i = jnp.concatenate([ti_p[:, :TOP_K], ti_s[:, :TOP_K]], axis=0)
        tok, dest, blk_e, n_used = _route(top_i, n, MOE_ROWS)
        yb = _moe_call(blk_e, n_used, f_all[tok], w_gu[l], b_gu[l].reshape(N_EXPERTS, 1, -1),
                       w_down[l], b_down[l].reshape(N_EXPERTS, 1, -1))
        dest = dest.reshape(n, TOP_K)
        xp = _combine_call(x1_p, mods_p[5], pr_p, [yb[dest[:n_p, k]] for k in range(TOP_K)], seq)
        xs = _combine_call(x1_s, mods_s[5], pr_s, [yb[dest[n_p:, k]] for k in range(TOP_K)], None)

        rows_p.append((ckv_p, kr_p, sbk_p, sbv_p))
        rows_s.append((ckv_s, kr_s, sbk_s, sbv_s))

    stack_p = lambda i, tail: jnp.stack([r[i] for r in rows_p]).reshape((depth, bsz, seq) + tail)
    stack_s = lambda i, tail: jnp.stack([r[i] for r in rows_s]).reshape((depth, nseq, t_new) + tail)
    return (xp.reshape(bsz, seq, d), xs.reshape(nseq, t_new, d),
            stack_p(0, (KV_RANK,)), stack_p(1, (ROPE_DIM,)),
            stack_p(2, (SB_HEADS, SB_DIM)), stack_p(3, (SB_HEADS, SB_DIM)),
            stack_s(0, (KV_RANK,)), stack_s(1, (ROPE_DIM,)),
            stack_s(2, (SB_HEADS, SB_DIM)), stack_s(3, (SB_HEADS, SB_DIM)))
```
